```python
import jax, jax.numpy as jnp
from jax import lax
import numpy as np


D_MODEL = 2048
BATCH = 1
SEQ = 8192
DEPTH = 4
DEC_BATCH = 2
DEC_SEQ = 4096
PAST_LEN = 128

HEAD_DIM = 128
DILATED_GROUPS = ((128, 1), (512, 4), (2048, 16))
A_HEADS_PER_GROUP = 4
A_HEADS = A_HEADS_PER_GROUP * len(DILATED_GROUPS)
A_OUT = A_HEADS_PER_GROUP * HEAD_DIM
B_Q_HEADS = 8
B_KV_HEADS = 2
B_HALF_WINDOW = 128
B_OUT = B_Q_HEADS * HEAD_DIM
C_HEADS = 8
C_OUT = C_HEADS * HEAD_DIM
GRID_W = 64
NA_ROWS = 8
NA_COLS = 16
T5_BUCKETS = 32
T5_MAX_DIST = 1024
T5_HEADS = A_HEADS + B_Q_HEADS
D_FF = 5632
N_BRANCHES = 3
EPS = 1e-6
NEG_INF = -1e30
IN_SPLITS = (A_HEADS * HEAD_DIM, A_HEADS * HEAD_DIM, A_HEADS * HEAD_DIM,
             B_Q_HEADS * HEAD_DIM, B_KV_HEADS * HEAD_DIM, B_KV_HEADS * HEAD_DIM,
             C_HEADS * HEAD_DIM, C_HEADS * HEAD_DIM, C_HEADS * HEAD_DIM,
             N_BRANCHES * D_MODEL)
D_IN = sum(IN_SPLITS)

kernel_name = "hybrid_dilated_window_neighbourhood_encoder"


def rms_norm(x, g):
    xf = x.astype(jnp.float32)
    y = xf * lax.rsqrt(jnp.mean(xf * xf, axis=-1, keepdims=True) + EPS)
    return (y * g.astype(jnp.float32)).astype(x.dtype)


def swiglu_ffn(x, g, w_in, w_out):
    h = rms_norm(x, g)
    gate, up = jnp.split(h @ w_in, 2, axis=-1)
    return (jax.nn.silu(gate) * up) @ w_out


def t5_bucket(rel):
    nb = T5_BUCKETS // 2
    max_exact = nb // 2
    sign = (rel > 0).astype(np.int32) * nb
    n = np.abs(rel)
    large = max_exact + (np.log(np.maximum(n, 1) / max_exact) / np.log(T5_MAX_DIST / max_exact)
                         * (nb - max_exact)).astype(np.int32)
    large = np.minimum(large, nb - 1)
    return sign + np.where(n < max_exact, n, large)


def band_attn(q, k, v, hw, bias):
    n, L, hq, dh = q.shape
    hkv = k.shape[2]
    rep = hq // hkv
    blk = hw
    nb = -(-L // blk)
    lp = nb * blk
    qb = jnp.pad(q, ((0, 0), (0, lp - L), (0, 0), (0, 0))).reshape(n, nb, blk, hkv, rep, dh)
    kv_pad = ((0, 0), (blk, lp - L + blk), (0, 0), (0, 0))
    idx = np.arange(nb)[:, None] * blk + np.arange(3 * blk)[None, :]
    kb = jnp.pad(k, kv_pad)[:, idx]
    vb = jnp.pad(v, kv_pad)[:, idx]
    qpos = np.arange(nb)[:, None] * blk + np.arange(blk)[None, :]
    kpos = idx - blk
    valid = ((np.abs(kpos[:, None, :] - qpos[:, :, None]) <= hw)
             & (kpos[:, None, :] >= 0) & (kpos[:, None, :] < L))
    s = jnp.einsum('nbqgrd,nbkgd->nbgrqk', qb, kb, preferred_element_type=jnp.float32) * (dh ** -0.5)
    s = s + bias.astype(jnp.float32).reshape(hkv, rep, blk, 3 * blk)
    s = jnp.where(valid[None, :, None, None], s, NEG_INF)
    m = jnp.max(s, axis=-1)
    p = jnp.exp(s - m[..., None])
    den = jnp.sum(p, axis=-1)
    num = jnp.einsum('nbgrqk,nbkgd->nbqgrd', p, vb.astype(jnp.float32))
    num = num.reshape(n, lp, hq, dh)[:, :L]
    m = m.transpose(0, 1, 4, 2, 3).reshape(n, lp, hq)[:, :L]
    den = den.transpose(0, 1, 4, 2, 3).reshape(n, lp, hq)[:, :L]
    return num, m, den


def to_residue(x, d):
    b, L = x.shape[:2]
    y = x.reshape((b, L // d, d) + x.shape[2:])
    return jnp.swapaxes(y, 1, 2).reshape((b * d, L // d) + x.shape[2:])


def from_residue(y, d, b):
    y = y.reshape((b, d) + y.shape[1:])
    return jnp.swapaxes(y, 1, 2).reshape((b, -1) + y.shape[3:])


def dilated_attn(q, k, v, rel_bias):
    b, L = q.shape[:2]
    nums, ms, dens = [], [], []
    for g, (w, d) in enumerate(DILATED_GROUPS):
        hw = w // (2 * d)
        sl = slice(g * A_HEADS_PER_GROUP, (g + 1) * A_HEADS_PER_GROUP)
        qg, kg, vg = to_residue(q[:, :, sl], d), to_residue(k[:, :, sl], d), to_residue(v[:, :, sl], d)
        rel = (np.arange(3 * hw)[None, :] - hw - np.arange(hw)[:, None]) * d
        bias = rel_bias[t5_bucket(rel)][:, :, sl].transpose(2, 0, 1)
        num, m, den = band_attn(qg, kg, vg, hw, bias)
        nums.append(from_residue(num, d, b))
        ms.append(from_residue(m, d, b))
        dens.append(from_residue(den, d, b))
    m_all = jnp.max(jnp.stack(ms), axis=0)
    wts = [jnp.exp(mg - m_all) for mg in ms]
    numer = sum(wg[..., None] * ng for wg, ng in zip(wts, nums))
    denom = sum(wg * dg for wg, dg in zip(wts, dens))
    return (numer / denom[..., None]).reshape(b, L, A_OUT)


def window_sink_attn(q, k, v, sink, rel_bias):
    b, L = q.shape[:2]
    hw = B_HALF_WINDOW
    rel = np.arange(3 * hw)[None, :] - hw - np.arange(hw)[:, None]
    bias = rel_bias[t5_bucket(rel)][:, :, A_HEADS:].transpose(2, 0, 1)
    num, m, den = band_attn(q, k, v, hw, bias)
    s = sink.astype(jnp.float32)
    m2 = jnp.maximum(m, s)
    a = jnp.exp(m - m2)
    o = num * a[..., None] / (den * a + jnp.exp(s - m2))[..., None]
    return o.reshape(b, L, B_OUT)


def neighborhood_attn(q, k, v, rpb):
    b, L, h, dh = q.shape
    rows = L // GRID_W
    kr = min(NA_ROWS, rows)
    kc = NA_COLS
    ncb = GRID_W // kc
    ck = 2 * kc
    r = np.arange(rows)
    rs = np.clip(r - kr // 2, 0, rows - kr)
    key_rows = rs[:, None] + np.arange(kr)[None, :]
    cbi = np.arange(ncb)
    cs = np.clip(cbi * kc - kc // 2, 0, GRID_W - ck)
    key_cols = cs[:, None] + np.arange(ck)[None, :]
    tok = (key_rows[:, None, :, None] * GRID_W + key_cols[None, :, None, :]).reshape(rows, ncb, kr * ck)
    kg = k[:, tok]
    vg = v[:, tok]
    qb = q.reshape(b, rows, ncb, kc, h, dh)
    qcol = cbi[:, None] * kc + np.arange(kc)[None, :]
    qstart = np.clip(qcol - kc // 2, 0, GRID_W - kc)
    colvalid = ((key_cols[:, None, :] >= qstart[:, :, None])
                & (key_cols[:, None, :] < qstart[:, :, None] + kc))
    mask = np.broadcast_to(colvalid[:, :, None, :], (ncb, kc, kr, ck)).reshape(ncb, kc, kr * ck)
    dr_idx = key_rows - r[:, None] + NA_ROWS - 1
    dc_idx = np.clip(key_cols[:, None, :] - qcol[:, :, None] + kc - 1, 0, 2 * kc - 2)
    bias = rpb[:, dr_idx[:, None, None, :, None], dc_idx[None, :, :, None, :]]
    bias = bias.reshape(h, rows, ncb, kc, kr * ck).transpose(1, 2, 0, 3, 4).astype(jnp.float32)
    s = jnp.einsum('brcqhd,brckhd->brchqk', qb, kg, preferred_element_type=jnp.float32) * (dh ** -0.5)
    s = jnp.where(mask[None, None, :, None], s + bias[None], NEG_INF)
    p = jax.nn.softmax(s, axis=-1)
    o = jnp.einsum('brchqk,brckhd->brcqhd', p, vg.astype(jnp.float32))
    return o.reshape(b, L, C_OUT)


def token_mixer(x, norm_g, w_in, qk_g, sink, rpb, rel_bias, w_ba, w_bb, w_bc, w_o):
    b, L, _ = x.shape
    h = rms_norm(x, norm_g)
    parts = jnp.split(h @ w_in, np.cumsum(IN_SPLITS)[:-1].tolist(), axis=-1)
    qa, ka, va, qb, kb, vb, qc, kc, vc = [t.reshape(b, L, -1, HEAD_DIM) for t in parts[:9]]
    qa, ka = rms_norm(qa, qk_g[0]), rms_norm(ka, qk_g[1])
    qb, kb = rms_norm(qb, qk_g[2]), rms_norm(kb, qk_g[3])
    qc, kc = rms_norm(qc, qk_g[4]), rms_norm(kc, qk_g[5])
    o_a = dilated_attn(qa, ka, va, rel_bias).astype(x.dtype)
    o_b = window_sink_attn(qb, kb, vb, sink, rel_bias).astype(x.dtype)
    o_c = neighborhood_attn(qc, kc, vc, rpb).astype(x.dtype)
    gates = jax.nn.sigmoid(parts[9].astype(jnp.float32)).astype(x.dtype).reshape(b, L, N_BRANCHES, D_MODEL)
    z = gates[:, :, 0] * (o_a @ w_ba) + gates[:, :, 1] * (o_b @ w_bb) + gates[:, :, 2] * (o_c @ w_bc)
    return z @ w_o


def run_trunk(x, rel_bias, ffn1_norm, ffn1_w_in, ffn1_w_out, mix_norm, w_in, qk_norm, sink, rpb,
              w_branch_a, w_branch_b, w_branch_c, w_out, ffn2_norm, ffn2_w_in, ffn2_w_out):
    for l in range(DEPTH):
        x = x + 0.5 * swiglu_ffn(x, ffn1_norm[l], ffn1_w_in[l], ffn1_w_out[l])
        x = x + token_mixer(x, mix_norm[l], w_in[l], qk_norm[l], sink[l], rpb[l], rel_bias,
                            w_branch_a[l], w_branch_b[l], w_branch_c[l], w_out[l])
        x = x + 0.5 * swiglu_ffn(x, ffn2_norm[l], ffn2_w_in[l], ffn2_w_out[l])
    return x


def setup_inputs(seed: int = 0) -> dict:
    key = jax.random.key(seed)
    ks = jax.random.split(key, 19)

    def nrm(k, shape, scale):
        return jax.random.normal(k, shape, jnp.float32) * scale

    return {
        "x_prompt": nrm(ks[0], (BATCH, SEQ, D_MODEL), 1.0),
        "x_sample": nrm(ks[1], (DEC_BATCH, DEC_SEQ, D_MODEL), 1.0),
        "rel_bias": nrm(ks[2], (T5_BUCKETS, T5_HEADS), 0.2),
        "ffn1_norm": 1.0 + nrm(ks[3], (DEPTH, D_MODEL), 0.05),
        "ffn1_w_in": nrm(ks[4], (DEPTH, D_MODEL, 2 * D_FF), D_MODEL ** -0.5),
        "ffn1_w_out": nrm(ks[5], (DEPTH, D_FF, D_MODEL), 0.5 * D_FF ** -0.5),
        "mix_norm": 1.0 + nrm(ks[6], (DEPTH, D_MODEL), 0.05),
        "w_in": nrm(ks[7], (DEPTH, D_MODEL, D_IN), D_MODEL ** -0.5),
        "qk_norm": 1.0 + nrm(ks[8], (DEPTH, 6, HEAD_DIM), 0.05),
        "sink": nrm(ks[9], (DEPTH, B_Q_HEADS), 0.5),
        "rpb": nrm(ks[10], (DEPTH, C_HEADS, 2 * NA_ROWS - 1, 2 * NA_COLS - 1), 0.2),
        "w_branch_a": nrm(ks[11], (DEPTH, A_OUT, D_MODEL), A_OUT ** -0.5),
        "w_branch_b": nrm(ks[12], (DEPTH, B_OUT, D_MODEL), B_OUT ** -0.5),
        "w_branch_c": nrm(ks[13], (DEPTH, C_OUT, D_MODEL), C_OUT ** -0.5),
        "w_out": nrm(ks[14], (DEPTH, D_MODEL, D_MODEL), 0.5 * D_MODEL ** -0.5),
        "ffn2_norm": 1.0 + nrm(ks[15], (DEPTH, D_MODEL), 0.05),
        "ffn2_w_in": nrm(ks[16], (DEPTH, D_MODEL, 2 * D_FF), D_MODEL ** -0.5),
        "ffn2_w_out": nrm(ks[17], (DEPTH, D_FF, D_MODEL), 0.5 * D_FF ** -0.5),
    }


def reference(x_prompt, x_sample, rel_bias, ffn1_norm, ffn1_w_in, ffn1_w_out, mix_norm, w_in, qk_norm,
              sink, rpb, w_branch_a, w_branch_b, w_branch_c, w_out, ffn2_norm, ffn2_w_in, ffn2_w_out):
    y_prompt = run_trunk(x_prompt, rel_bias, ffn1_norm, ffn1_w_in, ffn1_w_out, mix_norm, w_in, qk_norm,
                         sink, rpb, w_branch_a, w_branch_b, w_branch_c, w_out, ffn2_norm, ffn2_w_in,
                         ffn2_w_out)
    y_sample = run_trunk(x_sample, rel_bias, ffn1_norm, ffn1_w_in, ffn1_w_out, mix_norm, w_in, qk_norm,
                         sink, rpb, w_branch_a, w_branch_b, w_branch_c, w_out, ffn2_norm, ffn2_w_in,
                         ffn2_w_out)
    return (y_prompt, y_sample)
```

```python
import functools

import numpy as np
import jax
import jax.numpy as jnp
from jax import lax
from jax.experimental import pallas as pl
from jax.experimental.pallas import tpu as pltpu

F32 = jnp.float32
BF16 = jnp.bfloat16

D_MODEL = 2048
DEPTH = 4
HEAD_DIM = 128
DILATED_GROUPS = ((128, 1), (512, 4), (2048, 16))
A_HEADS_PER_GROUP = 4
A_HEADS = 12
A_OUT = 512
B_Q_HEADS = 8
B_KV_HEADS = 2
B_HALF_WINDOW = 128
B_OUT = 1024
C_HEADS = 8
C_OUT = 1024
GRID_W = 64
NA_ROWS = 8
NA_COLS = 16
T5_BUCKETS = 32
T5_MAX_DIST = 1024
D_FF = 5632
EPS = 1e-6
NEG_INF = -1e30
SM_SCALE = HEAD_DIM ** -0.5
QKV_COLS = 9216
N_QKV_HEADS = QKV_COLS // HEAD_DIM
GATE_COL0 = QKV_COLS

SPAN = 1024
A_HW = 64
VMEM_LIMIT = 56 * 1024 * 1024


def _t5_bucket(rel):
    nb = T5_BUCKETS // 2
    max_exact = nb // 2
    sign = (rel > 0).astype(np.int32) * nb
    n = np.abs(rel)
    large = max_exact + (np.log(np.maximum(n, 1) / max_exact) / np.log(T5_MAX_DIST / max_exact)
                         * (nb - max_exact)).astype(np.int32)
    large = np.minimum(large, nb - 1)
    return sign + np.where(n < max_exact, n, large)


def _rms(x, g):
    ms = jnp.mean(x * x, axis=-1, keepdims=True)
    return (x * lax.rsqrt(ms + EPS)) * g


def _ffn_kernel(x_ref, g_ref, wg_ref, wu_ref, wo_ref, o_ref, h_ref, *, nf):
    f = pl.program_id(1)

    @pl.when(f == 0)
    def _():
        h_ref[...] = _rms(x_ref[...], g_ref[...]).astype(BF16)
        o_ref[...] = jnp.zeros_like(o_ref)

    h = h_ref[...]
    gate = jnp.dot(h, wg_ref[...], preferred_element_type=F32)
    up = jnp.dot(h, wu_ref[...], preferred_element_type=F32)
    act = ((gate * jax.nn.sigmoid(gate)) * up).astype(BF16)
    o_ref[...] += jnp.dot(act, wo_ref[...], preferred_element_type=F32)

    @pl.when(f == nf - 1)
    def _():
        o_ref[...] = x_ref[...] + 0.5 * o_ref[...]


def _ffn(x, g, w_in, w_out, layer, *, tm=512, tf=512):
    t = x.shape[0]
    nf = D_FF // tf
    return pl.pallas_call(
        functools.partial(_ffn_kernel, nf=nf),
        grid=(t // tm, nf),
        in_specs=[
            pl.BlockSpec((tm, D_MODEL), lambda i, f: (i, 0)),
            pl.BlockSpec((None, 1, D_MODEL), lambda i, f: (layer, 0, 0)),
            pl.BlockSpec((None, D_MODEL, tf), lambda i, f: (layer, 0, f)),
            pl.BlockSpec((None, D_MODEL, tf), lambda i, f: (layer, 0, nf + f)),
            pl.BlockSpec((None, tf, D_MODEL), lambda i, f: (layer, f, 0)),
        ],
        out_specs=pl.BlockSpec((tm, D_MODEL), lambda i, f: (i, 0)),
        out_shape=jax.ShapeDtypeStruct((t, D_MODEL), F32),
        scratch_shapes=[pltpu.VMEM((tm, D_MODEL), BF16)],
        compiler_params=pltpu.CompilerParams(
            dimension_semantics=("parallel", "arbitrary"), vmem_limit_bytes=VMEM_LIMIT),
        name="ffn",
    )(x, g, w_in, w_in, w_out)


def _qkv_kernel(x_ref, g_ref, w_ref, gain_ref, flag_ref, o_ref, h_ref, stage_ref):
    n = pl.program_id(1)

    @pl.when(n == 0)
    def _():
        h_ref[...] = _rms(x_ref[...], g_ref[...]).astype(BF16)

    acc = jnp.dot(h_ref[...], w_ref[...], preferred_element_type=F32)
    for hh in range(4):
        sl = slice(hh * HEAD_DIM, (hh + 1) * HEAD_DIM)
        a = acc[:, sl]
        normed = _rms(a, gain_ref[:, sl])
        stage_ref[hh] = jnp.where(flag_ref[:, sl] > 0.0, normed, a)

    group = jnp.where(n < 9, n % 3, 0)

    @pl.when(group == 0)
    def _():
        o_ref[...] = stage_ref[...].astype(BF16)

    for gi, (_, d) in enumerate(DILATED_GROUPS):
        if d == 1:
            continue

        @pl.when(group == gi)
        def _(d=d):
            per = SPAN // d
            for hh in range(4):
                for r in range(d):
                    o_ref[hh, r * per:(r + 1) * per, :] = (
                        stage_ref[hh, pl.ds(r, per, stride=d), :].astype(BF16))


def _qkv_proj(x, g, w_in, gain, flag, layer, *, tn=512):
    t = x.shape[0]
    nn = QKV_COLS // tn
    hpb = tn // HEAD_DIM
    return pl.pallas_call(
        _qkv_kernel,
        grid=(t // SPAN, nn),
        in_specs=[
            pl.BlockSpec((SPAN, D_MODEL), lambda i, n: (i, 0)),
            pl.BlockSpec((None, 1, D_MODEL), lambda i, n: (layer, 0, 0)),
            pl.BlockSpec((None, D_MODEL, tn), lambda i, n: (layer, 0, n)),
            pl.BlockSpec((None, 1, tn), lambda i, n: (layer, 0, n)),
            pl.BlockSpec((1, tn), lambda i, n: (0, n)),
        ],
        out_specs=pl.BlockSpec((hpb, SPAN, HEAD_DIM), lambda i, n: (n, i, 0)),
        out_shape=jax.ShapeDtypeStruct((N_QKV_HEADS, t, HEAD_DIM), BF16),
        scratch_shapes=[pltpu.VMEM((SPAN, D_MODEL), BF16), pltpu.VMEM((hpb, SPAN, HEAD_DIM), F32)],
        compiler_params=pltpu.CompilerParams(
            dimension_semantics=("parallel", "arbitrary"), vmem_limit_bytes=VMEM_LIMIT),
        name="qkv_proj",
    )(x, g, w_in, gain, flag)


def _attn_a_kernel(flags_ref, *refs):
    q_refs = refs[0:3]
    k_refs = refs[3:12]
    v_refs = refs[12:21]
    bias_ref = refs[21]
    o_ref = refs[22]
    acc_ref, m_ref, l_ref = refs[23:26]

    s_idx = pl.program_id(0)
    is_first = flags_ref[0, s_idx] > 0
    is_last = flags_ref[1, s_idx] > 0
    lane = lax.broadcasted_iota(jnp.int32, (1, 3 * A_HW), 1)
    pen_first = jnp.where(jnp.logical_and(lane < A_HW, is_first), NEG_INF, 0.0).astype(F32)
    pen_last = jnp.where(jnp.logical_and(lane >= 2 * A_HW, is_last), NEG_INF, 0.0).astype(F32)
    nchunk = SPAN // A_HW

    for g, (_, d) in enumerate(DILATED_GROUPS):
        kp, kc, kn = k_refs[3 * g:3 * g + 3]
        vp, vc, vn = v_refs[3 * g:3 * g + 3]
        cpr = nchunk // d
        bias = bias_ref[g]
        for c in range(nchunk):
            r, jb = divmod(c, cpr)
            rows = slice(c * A_HW, (c + 1) * A_HW)
            if jb > 0:
                prev_k, prev_v, prev_rows, prev_out = kc, vc, slice((c - 1) * A_HW, c * A_HW), False
            else:
                pc = r * cpr + cpr - 1
                prev_k, prev_v, prev_rows, prev_out = kp, vp, slice(pc * A_HW, (pc + 1) * A_HW), True
            if jb < cpr - 1:
                next_k, next_v, next_rows, next_out = kc, vc, slice((c + 1) * A_HW, (c + 2) * A_HW), False
            else:
                nc = r * cpr
                next_k, next_v, next_rows, next_out = kn, vn, slice(nc * A_HW, (nc + 1) * A_HW), True
            q = q_refs[g][rows, :]
            kcat = jnp.concatenate([prev_k[prev_rows, :], kc[rows, :], next_k[next_rows, :]], axis=0)
            vcat = jnp.concatenate([prev_v[prev_rows, :], vc[rows, :], next_v[next_rows, :]], axis=0)
            s = lax.dot_general(q, kcat, (((1,), (1,)), ((), ())), preferred_element_type=F32)
            s = s * SM_SCALE + bias
            if prev_out:
                s = s + pen_first
            if next_out:
                s = s + pen_last
            m = jnp.max(s, axis=-1, keepdims=True)
            p = jnp.exp(s - m)
            l = jnp.sum(p, axis=-1, keepdims=True)
            acc = jnp.dot(p.astype(BF16), vcat, preferred_element_type=F32)
            if d == 1:
                dst = rows
            else:
                dst = pl.ds(jb * A_HW * d + r, A_HW, stride=d)
            acc_ref[g, dst, :] = acc
            m_ref[g, dst, :] = jnp.broadcast_to(m, (A_HW, HEAD_DIM))
            l_ref[g, dst, :] = jnp.broadcast_to(l, (A_HW, HEAD_DIM))

    m_all = jnp.maximum(jnp.maximum(m_ref[0], m_ref[1]), m_ref[2])
    w0 = jnp.exp(m_ref[0] - m_all)
    w1 = jnp.exp(m_ref[1] - m_all)
    w2 = jnp.exp(m_ref[2] - m_all)
    numer = w0 * acc_ref[0] + w1 * acc_ref[1] + w2 * acc_ref[2]
    denom = w0 * l_ref[0] + w1 * l_ref[1] + w2 * l_ref[2]
    o_ref[...] = (numer / denom).astype(BF16)


def _attn_a(qkv, bias, flags, nspan):
    blk = (None, SPAN, HEAD_DIM)

    def q_spec(g):
        return pl.BlockSpec(blk, lambda s, j, fl, g=g: (g * 4 + j, s, 0))

    def kv_specs(base, g):
        return [
            pl.BlockSpec(blk, lambda s, j, fl, g=g: (base + g * 4 + j, jnp.maximum(s - 1, 0), 0)),
            pl.BlockSpec(blk, lambda s, j, fl, g=g: (base + g * 4 + j, s, 0)),
            pl.BlockSpec(blk, lambda s, j, fl, g=g: (base + g * 4 + j, jnp.minimum(s + 1, nspan - 1), 0)),
        ]

    in_specs = [q_spec(g) for g in range(3)]
    for base in (A_HEADS, 2 * A_HEADS):
        for g in range(3):
            in_specs += kv_specs(base, g)
    in_specs.append(pl.BlockSpec((3, None, A_HW, 3 * A_HW), lambda s, j, fl: (0, j, 0, 0)))
    grid_spec = pltpu.PrefetchScalarGridSpec(
        num_scalar_prefetch=1,
        grid=(nspan, A_HEADS_PER_GROUP),
        in_specs=in_specs,
        out_specs=pl.BlockSpec((SPAN, HEAD_DIM), lambda s, j, fl: (s, j)),
        scratch_shapes=[pltpu.VMEM((3, SPAN, HEAD_DIM), F32)] * 3,
    )
    return pl.pallas_call(
        _attn_a_kernel,
        grid_spec=grid_spec,
        out_shape=jax.ShapeDtypeStruct((nspan * SPAN, A_OUT), BF16),
        compiler_params=pltpu.CompilerParams(
            dimension_semantics=("parallel", "arbitrary"), vmem_limit_bytes=VMEM_LIMIT),
        name="attn_dilated",
    )(flags, *([qkv] * 21), bias)


def _attn_b_kernel(flags_ref, q_ref, kp_ref, kc_ref, kn_ref, vp_ref, vc_ref, vn_ref, bias_ref, sink_ref, o_ref):
    hw = B_HALF_WINDOW
    rep = B_Q_HEADS // B_KV_HEADS
    s_idx = pl.program_id(0)
    is_first = flags_ref[0, s_idx] > 0
    is_last = flags_ref[1, s_idx] > 0
    lane = lax.broadcasted_iota(jnp.int32, (1, 3 * hw), 1)
    pen_first = jnp.where(jnp.logical_and(lane < hw, is_first), NEG_INF, 0.0).astype(F32)
    pen_last = jnp.where(jnp.logical_and(lane >= 2 * hw, is_last), NEG_INF, 0.0).astype(F32)
    kfull = jnp.concatenate([kp_ref[...], kc_ref[...], kn_ref[...]], axis=0)
    vfull = jnp.concatenate([vp_ref[...], vc_ref[...], vn_ref[...]], axis=0)
    sink = sink_ref[...]
    nblk = SPAN // hw
    for b in range(nblk):
        q = q_ref[:, b * hw:(b + 1) * hw, :].reshape(rep * hw, HEAD_DIM)
        kcat = kfull[b * hw:(b + 3) * hw, :]
        vcat = vfull[b * hw:(b + 3) * hw, :]
        s = lax.dot_general(q, kcat, (((1,), (1,)), ((), ())), preferred_element_type=F32)
        s = s * SM_SCALE + bias_ref[...]
        if b == 0:
            s = s + pen_first
        if b == nblk - 1:
            s = s + pen_last
        m = jnp.max(s, axis=-1, keepdims=True)
        p = jnp.exp(s - m)
        den = jnp.sum(p, axis=-1, keepdims=True)
        num = jnp.dot(p.astype(BF16), vcat, preferred_element_type=F32)
        m2 = jnp.maximum(m, sink)
        a = jnp.exp(m - m2)
        o = (num * a) / (den * a + jnp.exp(sink - m2))
        for hh in range(rep):
            o_ref[b * hw:(b + 1) * hw, hh * HEAD_DIM:(hh + 1) * HEAD_DIM] = (
                o[hh * hw:(hh + 1) * hw, :].astype(BF16))


def _attn_b(qkv, bias, sink_rows, flags, nspan):
    hw = B_HALF_WINDOW
    rep = B_Q_HEADS // B_KV_HEADS
    q0 = 3 * A_HEADS
    k0 = q0 + B_Q_HEADS
    v0 = k0 + B_KV_HEADS
    per = SPAN // hw
    nb_total = nspan * per

    def nbr_specs(base):
        return [
            pl.BlockSpec((None, hw, HEAD_DIM), lambda s, g, fl: (base + g, jnp.maximum(s * per - 1, 0), 0)),
            pl.BlockSpec((None, SPAN, HEAD_DIM), lambda s, g, fl: (base + g, s, 0)),
            pl.BlockSpec((None, hw, HEAD_DIM),
                         lambda s, g, fl: (base + g, jnp.minimum(s * per + per, nb_total - 1), 0)),
        ]

    in_specs = [pl.BlockSpec((rep, SPAN, HEAD_DIM), lambda s, g, fl: (q0 // rep + g, s, 0))]
    in_specs += nbr_specs(k0) + nbr_specs(v0)
    in_specs.append(pl.BlockSpec((None, rep * hw, 3 * hw), lambda s, g, fl: (g, 0, 0)))
    in_specs.append(pl.BlockSpec((None, rep * hw, 1), lambda s, g, fl: (g, 0, 0)))
    grid_spec = pltpu.PrefetchScalarGridSpec(
        num_scalar_prefetch=1,
        grid=(nspan, B_KV_HEADS),
        in_specs=in_specs,
        out_specs=pl.BlockSpec((SPAN, rep * HEAD_DIM), lambda s, g, fl: (s, g)),
    )
    return pl.pallas_call(
        _attn_b_kernel,
        grid_spec=grid_spec,
        out_shape=jax.ShapeDtypeStruct((nspan * SPAN, B_OUT), BF16),
        compiler_params=pltpu.CompilerParams(
            dimension_semantics=("parallel", "arbitrary"), vmem_limit_bytes=VMEM_LIMIT),
        name="attn_window",
    )(flags, qkv, qkv, qkv, qkv, qkv, qkv, qkv, bias, sink_rows)


C_KEYS = NA_ROWS * GRID_W
C_NBR = 256
ROWS_PER_SPAN = SPAN // GRID_W


def _attn_c_kernel(tab_ref, q_ref, kp_ref, kc_ref, kn_ref, vp_ref, vc_ref, vn_ref, bias_ref, o_ref,
                   kfull_ref, vfull_ref):
    s_idx = pl.program_id(0)
    kfull_ref[0:C_NBR, :] = kp_ref[...]
    kfull_ref[C_NBR:C_NBR + SPAN, :] = kc_ref[...]
    kfull_ref[C_NBR + SPAN:, :] = kn_ref[...]
    vfull_ref[0:C_NBR, :] = vp_ref[...]
    vfull_ref[C_NBR:C_NBR + SPAN, :] = vc_ref[...]
    vfull_ref[C_NBR + SPAN:, :] = vn_ref[...]
    for rr in range(ROWS_PER_SPAN):
        row = s_idx * ROWS_PER_SPAN + rr
        koff = pl.multiple_of(tab_ref[0, row], GRID_W)
        di = tab_ref[1, row]
        q = q_ref[rr * GRID_W:(rr + 1) * GRID_W, :]
        kcat = kfull_ref[pl.ds(koff, C_KEYS), :]
        vcat = vfull_ref[pl.ds(koff, C_KEYS), :]
        s = lax.dot_general(q, kcat, (((1,), (1,)), ((), ())), preferred_element_type=F32)
        s = s * SM_SCALE + bias_ref[di]
        m = jnp.max(s, axis=-1, keepdims=True)
        p = jnp.exp(s - m)
        l = jnp.sum(p, axis=-1, keepdims=True)
        p = p / l
        o = jnp.dot(p.astype(BF16), vcat, preferred_element_type=F32)
        o_ref[rr * GRID_W:(rr + 1) * GRID_W, :] = o.astype(BF16)


def _attn_c(qkv, bias, tab, nspan, layer):
    q0 = 3 * A_HEADS + B_Q_HEADS + 2 * B_KV_HEADS
    k0 = q0 + C_HEADS
    v0 = k0 + C_HEADS
    per = SPAN // C_NBR
    nb_total = nspan * per

    def nbr_specs(base):
        return [
            pl.BlockSpec((None, C_NBR, HEAD_DIM), lambda s, h, tb: (base + h, jnp.maximum(s * per - 1, 0), 0)),
            pl.BlockSpec((None, SPAN, HEAD_DIM), lambda s, h, tb: (base + h, s, 0)),
            pl.BlockSpec((None, C_NBR, HEAD_DIM),
                         lambda s, h, tb: (base + h, jnp.minimum(s * per + per, nb_total - 1), 0)),
        ]

    in_specs = [pl.BlockSpec((None, SPAN, HEAD_DIM), lambda s, h, tb: (q0 + h, s, 0))]
    in_specs += nbr_specs(k0) + nbr_specs(v0)
    in_specs.append(pl.BlockSpec((None, None, NA_ROWS, GRID_W, C_KEYS), lambda s, h, tb: (layer, h, 0, 0, 0)))
    grid_spec = pltpu.PrefetchScalarGridSpec(
        num_scalar_prefetch=1,
        grid=(nspan, C_HEADS),
        in_specs=in_specs,
        out_specs=pl.BlockSpec((SPAN, HEAD_DIM), lambda s, h, tb: (s, h)),
        scratch_shapes=[pltpu.VMEM((SPAN + 2 * C_NBR, HEAD_DIM), BF16)] * 2,
    )
    return pl.pallas_call(
        _attn_c_kernel,
        grid_spec=grid_spec,
        out_shape=jax.ShapeDtypeStruct((nspan * SPAN, C_OUT), BF16),
        compiler_params=pltpu.CompilerParams(
            dimension_semantics=("parallel", "arbitrary"), vmem_limit_bytes=VMEM_LIMIT),
        name="attn_nbr",
    )(tab, qkv, qkv, qkv, qkv, qkv, qkv, qkv, bias)


def _mix_kernel(x_ref, g_ref, oa_ref, ob_ref, oc_ref, wg0_ref, wg1_ref, wg2_ref, wba_ref, wbb_ref, wbc_ref,
                wo_ref, y_ref, h_ref, *, nc):
    c = pl.program_id(1)

    @pl.when(c == 0)
    def _():
        h_ref[...] = _rms(x_ref[...], g_ref[...]).astype(BF16)
        y_ref[...] = jnp.zeros_like(y_ref)

    h = h_ref[...]
    g0 = jax.nn.sigmoid(jnp.dot(h, wg0_ref[...], preferred_element_type=F32))
    g1 = jax.nn.sigmoid(jnp.dot(h, wg1_ref[...], preferred_element_type=F32))
    g2 = jax.nn.sigmoid(jnp.dot(h, wg2_ref[...], preferred_element_type=F32))
    pa = jnp.dot(oa_ref[...], wba_ref[...], preferred_element_type=F32)
    pb = jnp.dot(ob_ref[...], wbb_ref[...], preferred_element_type=F32)
    pc = jnp.dot(oc_ref[...], wbc_ref[...], preferred_element_type=F32)
    z = g0 * pa + g1 * pb + g2 * pc
    y_ref[...] += jnp.dot(z.astype(BF16), wo_ref[...], preferred_element_type=F32)

    @pl.when(c == nc - 1)
    def _():
        y_ref[...] = x_ref[...] + y_ref[...]


def _mix_out(x, g, o_a, o_b, o_c, w_in, w_ba, w_bb, w_bc, w_o, layer, *, tm=512, tn=512):
    t = x.shape[0]
    nc = D_MODEL // tn
    gate_blk = GATE_COL0 // tn

    def gate_spec(b):
        return pl.BlockSpec((None, D_MODEL, tn), lambda i, c, b=b: (layer, 0, gate_blk + b * nc + c))

    return pl.pallas_call(
        functools.partial(_mix_kernel, nc=nc),
        grid=(t // tm, nc),
        in_specs=[
            pl.BlockSpec((tm, D_MODEL), lambda i, c: (i, 0)),
            pl.BlockSpec((None, 1, D_MODEL), lambda i, c: (layer, 0, 0)),
            pl.BlockSpec((tm, A_OUT), lambda i, c: (i, 0)),
            pl.BlockSpec((tm, B_OUT), lambda i, c: (i, 0)),
            pl.BlockSpec((tm, C_OUT), lambda i, c: (i, 0)),
            gate_spec(0), gate_spec(1), gate_spec(2),
            pl.BlockSpec((None, A_OUT, tn), lambda i, c: (layer, 0, c)),
            pl.BlockSpec((None, B_OUT, tn), lambda i, c: (layer, 0, c)),
            pl.BlockSpec((None, C_OUT, tn), lambda i, c: (layer, 0, c)),
            pl.BlockSpec((None, tn, D_MODEL), lambda i, c: (layer, c, 0)),
        ],
        out_specs=pl.BlockSpec((tm, D_MODEL), lambda i, c: (i, 0)),
        out_shape=jax.ShapeDtypeStruct((t, D_MODEL), F32),
        scratch_shapes=[pltpu.VMEM((tm, D_MODEL), BF16)],
        compiler_params=pltpu.CompilerParams(
            dimension_semantics=("parallel", "arbitrary"), vmem_limit_bytes=VMEM_LIMIT),
        name="mix_out",
    )(x, g, o_a, o_b, o_c, w_in, w_in, w_in, w_ba, w_bb, w_bc, w_o)


def _bias_tables_ab(rel_bias):
    tabs = []
    for g, (w, d) in enumerate(DILATED_GROUPS):
        hw = w // (2 * d)
        off = np.arange(3 * hw)[None, :] - hw - np.arange(hw)[:, None]
        b = rel_bias[_t5_bucket(off * d)][:, :, g * 4:(g + 1) * 4].transpose(2, 0, 1)
        tabs.append(jnp.where(np.abs(off)[None] <= hw, b, NEG_INF))
    bias_a = jnp.stack(tabs).astype(F32)
    hw = B_HALF_WINDOW
    off = np.arange(3 * hw)[None, :] - hw - np.arange(hw)[:, None]
    b = rel_bias[_t5_bucket(off)][:, :, A_HEADS:].transpose(2, 0, 1)
    b = jnp.where(np.abs(off)[None] <= hw, b, NEG_INF).astype(F32)
    rep = B_Q_HEADS // B_KV_HEADS
    bias_b = b.reshape(B_KV_HEADS, rep * hw, 3 * hw)
    return bias_a, bias_b


def _bias_table_c(rpb):
    di = np.arange(NA_ROWS)
    dr_idx = di[:, None] + np.arange(NA_ROWS)[None, :]
    qcol = np.arange(GRID_W)
    kcol = np.arange(GRID_W)
    dc_idx = np.clip(kcol[None, :] - qcol[:, None] + NA_COLS - 1, 0, 2 * NA_COLS - 2)
    qstart = np.clip(qcol - NA_COLS // 2, 0, GRID_W - NA_COLS)
    valid = (kcol[None, :] >= qstart[:, None]) & (kcol[None, :] < qstart[:, None] + NA_COLS)
    b = rpb[:, :, dr_idx[:, None, :, None], dc_idx[None, :, None, :]]
    b = jnp.where(valid[None, None, None, :, None, :], b, NEG_INF)
    return b.reshape(DEPTH, C_HEADS, NA_ROWS, GRID_W, C_KEYS).astype(F32)


def _span_tables(seq_lens):
    nspan = sum(seq_lens) // SPAN
    first = np.zeros(nspan, np.int32)
    last = np.zeros(nspan, np.int32)
    koff, case = [], []
    s0 = 0
    for L in seq_lens:
        ns = L // SPAN
        first[s0] = 1
        last[s0 + ns - 1] = 1
        s0 += ns
        rows = L // GRID_W
        r = np.arange(rows)
        delta = np.clip(r - NA_ROWS // 2, 0, rows - NA_ROWS) - r
        koff.append((delta + r % ROWS_PER_SPAN) * GRID_W + C_NBR)
        case.append(delta + NA_ROWS - 1)
    flags = jnp.asarray(np.stack([first, last]))
    tab_c = jnp.asarray(np.stack([np.concatenate(koff), np.concatenate(case)]).astype(np.int32))
    return flags, tab_c, nspan


def _qkv_norm_tables(qk_norm):
    segs = [(A_HEADS, 0), (A_HEADS, 1), (A_HEADS, None), (B_Q_HEADS, 2), (B_KV_HEADS, 3), (B_KV_HEADS, None),
            (C_HEADS, 4), (C_HEADS, 5), (C_HEADS, None)]
    gains, flags = [], []
    for nh, idx in segs:
        if idx is None:
            gains.append(jnp.ones((DEPTH, nh * HEAD_DIM), F32))
            flags.append(np.zeros(nh * HEAD_DIM, np.float32))
        else:
            gains.append(jnp.tile(qk_norm[:, idx, :].astype(F32), (1, nh)))
            flags.append(np.ones(nh * HEAD_DIM, np.float32))
    gain = jnp.concatenate(gains, axis=1)[:, None, :]
    flag = jnp.asarray(np.concatenate(flags))[None, :]
    return gain, flag


def kernel(x_prompt, x_sample, rel_bias, ffn1_norm, ffn1_w_in, ffn1_w_out, mix_norm, w_in, qk_norm, sink, rpb,
           w_branch_a, w_branch_b, w_branch_c, w_out, ffn2_norm, ffn2_w_in, ffn2_w_out):
    seq_lens = (x_prompt.shape[1],) * x_prompt.shape[0] + (x_sample.shape[1],) * x_sample.shape[0]
    assert all(L % SPAN == 0 for L in seq_lens)
    x = jnp.concatenate([x_prompt.reshape(-1, D_MODEL), x_sample.reshape(-1, D_MODEL)], axis=0)

    flags, tab_c, nspan = _span_tables(seq_lens)
    bias_a, bias_b = _bias_tables_ab(rel_bias.astype(F32))
    bias_c = _bias_table_c(rpb.astype(F32))
    qk_gain, qk_flag = _qkv_norm_tables(qk_norm)
    rep = B_Q_HEADS // B_KV_HEADS
    sink_rows = jnp.repeat(sink.astype(F32), B_HALF_WINDOW, axis=1).reshape(
        DEPTH, B_KV_HEADS, rep * B_HALF_WINDOW, 1)

    bf = lambda w: w.astype(BF16)
    f1_in, f1_out, f2_in, f2_out = bf(ffn1_w_in), bf(ffn1_w_out), bf(ffn2_w_in), bf(ffn2_w_out)
    w_in_b, w_ba, w_bb, w_bc, w_o = bf(w_in), bf(w_branch_a), bf(w_branch_b), bf(w_branch_c), bf(w_out)
    n1 = ffn1_norm.astype(F32)[:, None, :]
    nm = mix_norm.astype(F32)[:, None, :]
    n2 = ffn2_norm.astype(F32)[:, None, :]

    for l in range(DEPTH):
        x = _ffn(x, n1, f1_in, f1_out, l)
        qkv = _qkv_proj(x, nm, w_in_b, qk_gain, qk_flag, l)
        o_a = _attn_a(qkv, bias_a, flags, nspan)
        o_b = _attn_b(qkv, bias_b, sink_rows[l], flags, nspan)
        o_c = _attn_c(qkv, bias_c, tab_c, nspan, l)
        x = _mix_out(x, nm, o_a, o_b, o_c, w_in_b, w_ba, w_bb, w_bc, w_o, l)
        x = _ffn(x, n2, f2_in, f2_out, l)

    n_p = x_prompt.shape[0] * x_prompt.shape[1]
    return (x[:n_p].reshape(x_prompt.shape), x[n_p:].reshape(x_sample.shape))
```

```python
import functools

import numpy as np
import jax
import jax.numpy as jnp
from jax import lax
from jax.experimental import pallas as pl
from jax.experimental.pallas import tpu as pltpu

F32 = jnp.float32
BF16 = jnp.bfloat16

D_MODEL = 2048
DEPTH = 4
HEAD_DIM = 128
DILATED_GROUPS = ((128, 1), (512, 4), (2048, 16))
A_HEADS_PER_GROUP = 4
A_HEADS = 12
A_OUT = 512
B_Q_HEADS = 8
B_KV_HEADS = 2
B_HALF_WINDOW = 128
B_OUT = 1024
C_HEADS = 8
C_OUT = 1024
GRID_W = 64
NA_ROWS = 8
NA_COLS = 16
T5_BUCKETS = 32
T5_MAX_DIST = 1024
D_FF = 5632
EPS = 1e-6
NEG_INF = -1e30
SM_SCALE = HEAD_DIM ** -0.5
QKV_COLS = 9216
N_QKV_HEADS = QKV_COLS // HEAD_DIM
GATE_COL0 = QKV_COLS

SPAN = 1024
A_HW = 64
VMEM_LIMIT = 56 * 1024 * 1024


def _t5_bucket(rel):
    nb = T5_BUCKETS // 2
    max_exact = nb // 2
    sign = (rel > 0).astype(np.int32) * nb
    n = np.abs(rel)
    large = max_exact + (np.log(np.maximum(n, 1) / max_exact) / np.log(T5_MAX_DIST / max_exact)
                         * (nb - max_exact)).astype(np.int32)
    large = np.minimum(large, nb - 1)
    return sign + np.where(n < max_exact, n, large)


def _rms(x, g):
    ms = jnp.mean(x * x, axis=-1, keepdims=True)
    return (x * lax.rsqrt(ms + EPS)) * g


def _skewed(n, first, second, skew):
    pending = [first(i) for i in range(min(skew, n))]
    for i in range(n):
        cur = pending.pop(0)
        if i + skew < n:
            pending.append(first(i + skew))
        second(i, cur)


def _ffn_kernel(x_ref, g_ref, wg_ref, wu_ref, wo_ref, o_ref, h_ref, *, nf):
    f = pl.program_id(1)

    @pl.when(f == 0)
    def _():
        h_ref[...] = _rms(x_ref[...], g_ref[...]).astype(BF16)
        o_ref[...] = jnp.zeros_like(o_ref)

    h = h_ref[...]
    gate = jnp.dot(h, wg_ref[...], preferred_element_type=F32)
    up = jnp.dot(h, wu_ref[...], preferred_element_type=F32)
    act = ((gate * jax.nn.sigmoid(gate)) * up).astype(BF16)
    o_ref[...] += jnp.dot(act, wo_ref[...], preferred_element_type=F32)

    @pl.when(f == nf - 1)
    def _():
        o_ref[...] = x_ref[...] + 0.5 * o_ref[...]


def _ffn(x, g, w_in, w_out, layer, *, tm=1024, tf=512):
    t = x.shape[0]
    nf = D_FF // tf
    return pl.pallas_call(
        functools.partial(_ffn_kernel, nf=nf),
        grid=(t // tm, nf),
        in_specs=[
            pl.BlockSpec((tm, D_MODEL), lambda i, f: (i, 0)),
            pl.BlockSpec((None, 1, D_MODEL), lambda i, f: (layer, 0, 0)),
            pl.BlockSpec((None, D_MODEL, tf), lambda i, f: (layer, 0, f)),
            pl.BlockSpec((None, D_MODEL, tf), lambda i, f: (layer, 0, nf + f)),
            pl.BlockSpec((None, tf, D_MODEL), lambda i, f: (layer, f, 0)),
        ],
        out_specs=pl.BlockSpec((tm, D_MODEL), lambda i, f: (i, 0)),
        out_shape=jax.ShapeDtypeStruct((t, D_MODEL), F32),
        scratch_shapes=[pltpu.VMEM((tm, D_MODEL), BF16)],
        compiler_params=pltpu.CompilerParams(
            dimension_semantics=("parallel", "arbitrary"), vmem_limit_bytes=VMEM_LIMIT),
        name="ffn",
    )(x, g, w_in, w_in, w_out)


def _qkv_kernel(x_ref, g_ref, w_ref, gain_ref, flag_ref, o_ref, h_ref, stage_ref):
    n = pl.program_id(1)

    @pl.when(n == 0)
    def _():
        h_ref[...] = _rms(x_ref[...], g_ref[...]).astype(BF16)

    nchunk = 4
    rows_per = SPAN // nchunk

    def project(c):
        return jnp.dot(h_ref[c * rows_per:(c + 1) * rows_per, :], w_ref[...], preferred_element_type=F32)

    def head_norm(c, acc):
        rows = slice(c * rows_per, (c + 1) * rows_per)
        for hh in range(4):
            sl = slice(hh * HEAD_DIM, (hh + 1) * HEAD_DIM)
            a = acc[:, sl]
            normed = _rms(a, gain_ref[:, sl])
            res = jnp.where(flag_ref[:, sl] > 0.0, normed, a)
            stage_ref[hh, rows, :] = res
            o_ref[hh, rows, :] = res.astype(BF16)

    _skewed(nchunk, project, head_norm, skew=1)

    group = jnp.where(n < 9, n % 3, 0)

    for gi, (_, d) in enumerate(DILATED_GROUPS):
        if d == 1:
            continue

        @pl.when(group == gi)
        def _(d=d):
            per = SPAN // d
            for hh in range(4):
                for r in range(d):
                    o_ref[hh, r * per:(r + 1) * per, :] = (
                        stage_ref[hh, pl.ds(r, per, stride=d), :].astype(BF16))


def _qkv_proj(x, g, w_in, gain, flag, layer, *, tn=512):
    t = x.shape[0]
    nn = QKV_COLS // tn
    hpb = tn // HEAD_DIM
    return pl.pallas_call(
        _qkv_kernel,
        grid=(t // SPAN, nn),
        in_specs=[
            pl.BlockSpec((SPAN, D_MODEL), lambda i, n: (i, 0)),
            pl.BlockSpec((None, 1, D_MODEL), lambda i, n: (layer, 0, 0)),
            pl.BlockSpec((None, D_MODEL, tn), lambda i, n: (layer, 0, n)),
            pl.BlockSpec((None, 1, tn), lambda i, n: (layer, 0, n)),
            pl.BlockSpec((1, tn), lambda i, n: (0, n)),
        ],
        out_specs=pl.BlockSpec((hpb, SPAN, HEAD_DIM), lambda i, n: (n, i, 0)),
        out_shape=jax.ShapeDtypeStruct((N_QKV_HEADS, t, HEAD_DIM), BF16),
        scratch_shapes=[pltpu.VMEM((SPAN, D_MODEL), BF16), pltpu.VMEM((hpb, SPAN, HEAD_DIM), F32)],
        compiler_params=pltpu.CompilerParams(
            dimension_semantics=("parallel", "arbitrary"), vmem_limit_bytes=VMEM_LIMIT),
        name="qkv_proj",
    )(x, g, w_in, gain, flag)


def _attn_a_kernel(flags_ref, *refs):
    q_refs = refs[0:3]
    k_refs = refs[3:12]
    v_refs = refs[12:21]
    bias_ref = refs[21]
    o_ref = refs[22]
    acc_ref, m_ref, l_ref = refs[23:26]

    s_idx = pl.program_id(0)
    is_first = flags_ref[0, s_idx] > 0
    is_last = flags_ref[1, s_idx] > 0
    lane = lax.broadcasted_iota(jnp.int32, (1, 3 * A_HW), 1)
    pen_first = jnp.where(jnp.logical_and(lane < A_HW, is_first), NEG_INF, 0.0).astype(F32)
    pen_last = jnp.where(jnp.logical_and(lane >= 2 * A_HW, is_last), NEG_INF, 0.0).astype(F32)
    nchunk = SPAN // A_HW

    def chunk_plan(idx):
        g, c = divmod(idx, nchunk)
        d = DILATED_GROUPS[g][1]
        cpr = nchunk // d
        r, jb = divmod(c, cpr)
        prev = (1, c - 1, False) if jb > 0 else (0, r * cpr + cpr - 1, True)
        nxt = (1, c + 1, False) if jb < cpr - 1 else (2, r * cpr, True)
        return g, d, c, r, jb, prev, nxt

    def cat3(refs, g, c, prev, nxt):
        def chunk(which, cc):
            return refs[3 * g + which][cc * A_HW:(cc + 1) * A_HW, :]
        return jnp.concatenate([chunk(prev[0], prev[1]), chunk(1, c), chunk(nxt[0], nxt[1])], axis=0)

    def scores(idx):
        g, d, c, r, jb, prev, nxt = chunk_plan(idx)
        q = q_refs[g][c * A_HW:(c + 1) * A_HW, :]
        kcat = cat3(k_refs, g, c, prev, nxt)
        s = lax.dot_general(q, kcat, (((1,), (1,)), ((), ())), preferred_element_type=F32)
        s = s * SM_SCALE + bias_ref[g]
        if prev[2]:
            s = s + pen_first
        if nxt[2]:
            s = s + pen_last
        return s

    def finish(idx, s):
        g, d, c, r, jb, prev, nxt = chunk_plan(idx)
        vcat = cat3(v_refs, g, c, prev, nxt)
        m = jnp.max(s, axis=-1, keepdims=True)
        p = jnp.exp(s - m)
        l = jnp.sum(p, axis=-1, keepdims=True)
        acc = jnp.dot(p.astype(BF16), vcat, preferred_element_type=F32)
        if d == 1:
            dst = slice(c * A_HW, (c + 1) * A_HW)
        else:
            dst = pl.ds(jb * A_HW * d + r, A_HW, stride=d)
        acc_ref[g, dst, :] = acc
        m_ref[g, dst, :] = jnp.broadcast_to(m, (A_HW, HEAD_DIM))
        l_ref[g, dst, :] = jnp.broadcast_to(l, (A_HW, HEAD_DIM))

    _skewed(len(DILATED_GROUPS) * nchunk, scores, finish, skew=12)

    m_all = jnp.maximum(jnp.maximum(m_ref[0], m_ref[1]), m_ref[2])
    w0 = jnp.exp(m_ref[0] - m_all)
    w1 = jnp.exp(m_ref[1] - m_all)
    w2 = jnp.exp(m_ref[2] - m_all)
    numer = w0 * acc_ref[0] + w1 * acc_ref[1] + w2 * acc_ref[2]
    denom = w0 * l_ref[0] + w1 * l_ref[1] + w2 * l_ref[2]
    o_ref[...] = (numer / denom).astype(BF16)


def _attn_a(qkv, bias, flags, nspan):
    blk = (None, SPAN, HEAD_DIM)

    def q_spec(g):
        return pl.BlockSpec(blk, lambda s, j, fl, g=g: (g * 4 + j, s, 0))

    def kv_specs(base, g):
        return [
            pl.BlockSpec(blk, lambda s, j, fl, g=g: (base + g * 4 + j, jnp.maximum(s - 1, 0), 0)),
            pl.BlockSpec(blk, lambda s, j, fl, g=g: (base + g * 4 + j, s, 0)),
            pl.BlockSpec(blk, lambda s, j, fl, g=g: (base + g * 4 + j, jnp.minimum(s + 1, nspan - 1), 0)),
        ]

    in_specs = [q_spec(g) for g in range(3)]
    for base in (A_HEADS, 2 * A_HEADS):
        for g in range(3):
            in_specs += kv_specs(base, g)
    in_specs.append(pl.BlockSpec((3, None, A_HW, 3 * A_HW), lambda s, j, fl: (0, j, 0, 0)))
    grid_spec = pltpu.PrefetchScalarGridSpec(
        num_scalar_prefetch=1,
        grid=(nspan, A_HEADS_PER_GROUP),
        in_specs=in_specs,
        out_specs=pl.BlockSpec((SPAN, HEAD_DIM), lambda s, j, fl: (s, j)),
        scratch_shapes=[pltpu.VMEM((3, SPAN, HEAD_DIM), F32)] * 3,
    )
    return pl.pallas_call(
        _attn_a_kernel,
        grid_spec=grid_spec,
        out_shape=jax.ShapeDtypeStruct((nspan * SPAN, A_OUT), BF16),
        compiler_params=pltpu.CompilerParams(
            dimension_semantics=("parallel", "arbitrary"), vmem_limit_bytes=VMEM_LIMIT),
        name="attn_dilated",
    )(flags, *([qkv] * 21), bias)


def _attn_b_kernel(flags_ref, q_ref, kp_ref, kc_ref, kn_ref, vp_ref, vc_ref, vn_ref, bias_ref, sink_ref, o_ref):
    hw = B_HALF_WINDOW
    rep = B_Q_HEADS // B_KV_HEADS
    s_idx = pl.program_id(0)
    is_first = flags_ref[0, s_idx] > 0
    is_last = flags_ref[1, s_idx] > 0
    lane = lax.broadcasted_iota(jnp.int32, (1, 3 * hw), 1)
    pen_first = jnp.where(jnp.logical_and(lane < hw, is_first), NEG_INF, 0.0).astype(F32)
    pen_last = jnp.where(jnp.logical_and(lane >= 2 * hw, is_last), NEG_INF, 0.0).astype(F32)
    kfull = jnp.concatenate([kp_ref[...], kc_ref[...], kn_ref[...]], axis=0)
    vfull = jnp.concatenate([vp_ref[...], vc_ref[...], vn_ref[...]], axis=0)
    nblk = SPAN // hw
    wide = (rep * hw, HEAD_DIM)

    def scores(b):
        q = q_ref[:, b * hw:(b + 1) * hw, :].reshape(rep * hw, HEAD_DIM)
        kcat = kfull[b * hw:(b + 3) * hw, :]
        s = lax.dot_general(q, kcat, (((1,), (1,)), ((), ())), preferred_element_type=F32)
        s = s * SM_SCALE + bias_ref[...]
        if b == 0:
            s = s + pen_first
        if b == nblk - 1:
            s = s + pen_last
        return s

    def finish(b, s):
        vcat = vfull[b * hw:(b + 3) * hw, :]
        m = jnp.max(s, axis=-1, keepdims=True)
        p = jnp.exp(s - m)
        den = jnp.broadcast_to(jnp.sum(p, axis=-1, keepdims=True), wide)
        num = jnp.dot(p.astype(BF16), vcat, preferred_element_type=F32)
        m = jnp.broadcast_to(m, wide)
        sink = sink_ref[...]
        m2 = jnp.maximum(m, sink)
        a = jnp.exp(m - m2)
        o = (num * a) / (den * a + jnp.exp(sink - m2))
        for hh in range(rep):
            o_ref[b * hw:(b + 1) * hw, hh * HEAD_DIM:(hh + 1) * HEAD_DIM] = (
                o[hh * hw:(hh + 1) * hw, :].astype(BF16))

    _skewed(nblk, scores, finish, skew=2)


def _attn_b(qkv, bias, sink_rows, flags, nspan):
    hw = B_HALF_WINDOW
    rep = B_Q_HEADS // B_KV_HEADS
    q0 = 3 * A_HEADS
    k0 = q0 + B_Q_HEADS
    v0 = k0 + B_KV_HEADS
    per = SPAN // hw
    nb_total = nspan * per

    def nbr_specs(base):
        return [
            pl.BlockSpec((None, hw, HEAD_DIM), lambda s, g, fl: (base + g, jnp.maximum(s * per - 1, 0), 0)),
            pl.BlockSpec((None, SPAN, HEAD_DIM), lambda s, g, fl: (base + g, s, 0)),
            pl.BlockSpec((None, hw, HEAD_DIM),
                         lambda s, g, fl: (base + g, jnp.minimum(s * per + per, nb_total - 1), 0)),
        ]

    in_specs = [pl.BlockSpec((rep, SPAN, HEAD_DIM), lambda s, g, fl: (q0 // rep + g, s, 0))]
    in_specs += nbr_specs(k0) + nbr_specs(v0)
    in_specs.append(pl.BlockSpec((None, rep * hw, 3 * hw), lambda s, g, fl: (g, 0, 0)))
    in_specs.append(pl.BlockSpec((None, rep * hw, HEAD_DIM), lambda s, g, fl: (g, 0, 0)))
    grid_spec = pltpu.PrefetchScalarGridSpec(
        num_scalar_prefetch=1,
        grid=(nspan, B_KV_HEADS),
        in_specs=in_specs,
        out_specs=pl.BlockSpec((SPAN, rep * HEAD_DIM), lambda s, g, fl: (s, g)),
    )
    return pl.pallas_call(
        _attn_b_kernel,
        grid_spec=grid_spec,
        out_shape=jax.ShapeDtypeStruct((nspan * SPAN, B_OUT), BF16),
        compiler_params=pltpu.CompilerParams(
            dimension_semantics=("parallel", "arbitrary"), vmem_limit_bytes=VMEM_LIMIT),
        name="attn_window",
    )(flags, qkv, qkv, qkv, qkv, qkv, qkv, qkv, bias, sink_rows)


C_KEYS = NA_ROWS * GRID_W
C_NBR = 256
ROWS_PER_SPAN = SPAN // GRID_W


def _attn_c_kernel(tab_ref, q_ref, kp_ref, kc_ref, kn_ref, vp_ref, vc_ref, vn_ref, bias_ref, o_ref,
                   kfull_ref, vfull_ref):
    s_idx = pl.program_id(0)
    kfull_ref[0:C_NBR, :] = kp_ref[...]
    kfull_ref[C_NBR:C_NBR + SPAN, :] = kc_ref[...]
    kfull_ref[C_NBR + SPAN:, :] = kn_ref[...]
    vfull_ref[0:C_NBR, :] = vp_ref[...]
    vfull_ref[C_NBR:C_NBR + SPAN, :] = vc_ref[...]
    vfull_ref[C_NBR + SPAN:, :] = vn_ref[...]
    def scores(rr):
        row = s_idx * ROWS_PER_SPAN + rr
        koff = pl.multiple_of(tab_ref[0, row], GRID_W)
        q = q_ref[rr * GRID_W:(rr + 1) * GRID_W, :]
        kcat = kfull_ref[pl.ds(koff, C_KEYS), :]
        s = lax.dot_general(q, kcat, (((1,), (1,)), ((), ())), preferred_element_type=F32)
        di = tab_ref[1, row]
        bias = jnp.concatenate([bias_ref[di + 2 * kk] for kk in range(NA_ROWS // 2)], axis=1)
        return s * SM_SCALE + bias

    def finish(rr, s):
        row = s_idx * ROWS_PER_SPAN + rr
        koff = pl.multiple_of(tab_ref[0, row], GRID_W)
        vcat = vfull_ref[pl.ds(koff, C_KEYS), :]
        m = jnp.max(s, axis=-1, keepdims=True)
        p = jnp.exp(s - m)
        l = jnp.sum(p, axis=-1, keepdims=True)
        p = p / l
        o = jnp.dot(p.astype(BF16), vcat, preferred_element_type=F32)
        o_ref[rr * GRID_W:(rr + 1) * GRID_W, :] = o.astype(BF16)

    _skewed(ROWS_PER_SPAN, scores, finish, skew=16)


def _attn_c(qkv, bias, tab, nspan, layer):
    q0 = 3 * A_HEADS + B_Q_HEADS + 2 * B_KV_HEADS
    k0 = q0 + C_HEADS
    v0 = k0 + C_HEADS
    per = SPAN // C_NBR
    nb_total = nspan * per

    def nbr_specs(base):
        return [
            pl.BlockSpec((None, C_NBR, HEAD_DIM), lambda s, h, tb: (base + h, jnp.maximum(s * per - 1, 0), 0)),
            pl.BlockSpec((None, SPAN, HEAD_DIM), lambda s, h, tb: (base + h, s, 0)),
            pl.BlockSpec((None, C_NBR, HEAD_DIM),
                         lambda s, h, tb: (base + h, jnp.minimum(s * per + per, nb_total - 1), 0)),
        ]

    in_specs = [pl.BlockSpec((None, SPAN, HEAD_DIM), lambda s, h, tb: (q0 + h, s, 0))]
    in_specs += nbr_specs(k0) + nbr_specs(v0)
    in_specs.append(pl.BlockSpec((None, None, 2 * NA_ROWS - 2, GRID_W, 2 * GRID_W),
                                 lambda s, h, tb: (layer, h, 0, 0, 0)))
    grid_spec = pltpu.PrefetchScalarGridSpec(
        num_scalar_prefetch=1,
        grid=(nspan, C_HEADS),
        in_specs=in_specs,
        out_specs=pl.BlockSpec((SPAN, HEAD_DIM), lambda s, h, tb: (s, h)),
        scratch_shapes=[pltpu.VMEM((SPAN + 2 * C_NBR, HEAD_DIM), BF16)] * 2,
    )
    return pl.pallas_call(
        _attn_c_kernel,
        grid_spec=grid_spec,
        out_shape=jax.ShapeDtypeStruct((nspan * SPAN, C_OUT), BF16),
        compiler_params=pltpu.CompilerParams(
            dimension_semantics=("parallel", "arbitrary"), vmem_limit_bytes=VMEM_LIMIT),
        name="attn_nbr",
    )(tab, qkv, qkv, qkv, qkv, qkv, qkv, qkv, bias)


def _mix_kernel(x_ref, g_ref, oa_ref, ob_ref, oc_ref, wg0_ref, wg1_ref, wg2_ref, wba_ref, wbb_ref, wbc_ref,
                wo_ref, y_ref, h_ref, *, nc):
    c = pl.program_id(1)

    @pl.when(c == 0)
    def _():
        h_ref[...] = _rms(x_ref[...], g_ref[...]).astype(BF16)
        y_ref[...] = jnp.zeros_like(y_ref)

    h = h_ref[...]
    g0 = jax.nn.sigmoid(jnp.dot(h, wg0_ref[...], preferred_element_type=F32))
    g1 = jax.nn.sigmoid(jnp.dot(h, wg1_ref[...], preferred_element_type=F32))
    g2 = jax.nn.sigmoid(jnp.dot(h, wg2_ref[...], preferred_element_type=F32))
    pa = jnp.dot(oa_ref[...], wba_ref[...], preferred_element_type=F32)
    pb = jnp.dot(ob_ref[...], wbb_ref[...], preferred_element_type=F32)
    pc = jnp.dot(oc_ref[...], wbc_ref[...], preferred_element_type=F32)
    z = g0 * pa + g1 * pb + g2 * pc
    y_ref[...] += jnp.dot(z.astype(BF16), wo_ref[...], preferred_element_type=F32)

    @pl.when(c == nc - 1)
    def _():
        y_ref[...] = x_ref[...] + y_ref[...]


def _mix_out(x, g, o_a, o_b, o_c, w_in, w_ba, w_bb, w_bc, w_o, layer, *, tm=512, tn=512):
    t = x.shape[0]
    nc = D_MODEL // tn
    gate_blk = GATE_COL0 // tn

    def gate_spec(b):
        return pl.BlockSpec((None, D_MODEL, tn), lambda i, c, b=b: (layer, 0, gate_blk + b * nc + c))

    return pl.pallas_call(
        functools.partial(_mix_kernel, nc=nc),
        grid=(t // tm, nc),
        in_specs=[
            pl.BlockSpec((tm, D_MODEL), lambda i, c: (i, 0)),
            pl.BlockSpec((None, 1, D_MODEL), lambda i, c: (layer, 0, 0)),
            pl.BlockSpec((tm, A_OUT), lambda i, c: (i, 0)),
            pl.BlockSpec((tm, B_OUT), lambda i, c: (i, 0)),
            pl.BlockSpec((tm, C_OUT), lambda i, c: (i, 0)),
            gate_spec(0), gate_spec(1), gate_spec(2),
            pl.BlockSpec((None, A_OUT, tn), lambda i, c: (layer, 0, c)),
            pl.BlockSpec((None, B_OUT, tn), lambda i, c: (layer, 0, c)),
            pl.BlockSpec((None, C_OUT, tn), lambda i, c: (layer, 0, c)),
            pl.BlockSpec((None, tn, D_MODEL), lambda i, c: (layer, c, 0)),
        ],
        out_specs=pl.BlockSpec((tm, D_MODEL), lambda i, c: (i, 0)),
        out_shape=jax.ShapeDtypeStruct((t, D_MODEL), F32),
        scratch_shapes=[pltpu.VMEM((tm, D_MODEL), BF16)],
        compiler_params=pltpu.CompilerParams(
            dimension_semantics=("parallel", "arbitrary"), vmem_limit_bytes=VMEM_LIMIT),
        name="mix_out",
    )(x, g, o_a, o_b, o_c, w_in, w_in, w_in, w_ba, w_bb, w_bc, w_o)


def _bias_tables_ab(rel_bias):
    tabs = []
    for g, (w, d) in enumerate(DILATED_GROUPS):
        hw = w // (2 * d)
        off = np.arange(3 * hw)[None, :] - hw - np.arange(hw)[:, None]
        b = rel_bias[_t5_bucket(off * d)][:, :, g * 4:(g + 1) * 4].transpose(2, 0, 1)
        tabs.append(jnp.where(np.abs(off)[None] <= hw, b, NEG_INF))
    bias_a = jnp.stack(tabs).astype(F32)
    hw = B_HALF_WINDOW
    off = np.arange(3 * hw)[None, :] - hw - np.arange(hw)[:, None]
    b = rel_bias[_t5_bucket(off)][:, :, A_HEADS:].transpose(2, 0, 1)
    b = jnp.where(np.abs(off)[None] <= hw, b, NEG_INF).astype(F32)
    rep = B_Q_HEADS // B_KV_HEADS
    bias_b = b.reshape(B_KV_HEADS, rep * hw, 3 * hw)
    return bias_a, bias_b


def _bias_table_c(rpb):
    ndr = 2 * NA_ROWS - 1
    ext = GRID_W - NA_COLS
    lead = jnp.broadcast_to(rpb[..., :1], rpb.shape[:-1] + (ext,))
    tail = jnp.broadcast_to(rpb[..., -1:], rpb.shape[:-1] + (ext + 1,))
    e = jnp.concatenate([lead, rpb, tail], axis=-1)
    period = 2 * GRID_W
    flat = jnp.tile(e, (1, 1, 1, GRID_W))[..., :GRID_W * (period - 1)]
    toep = flat.reshape(DEPTH, C_HEADS, ndr, GRID_W, period - 1)[..., GRID_W - 1:]
    qcol = np.arange(GRID_W)
    kcol = np.arange(GRID_W)
    qstart = np.clip(qcol - NA_COLS // 2, 0, GRID_W - NA_COLS)
    valid = (kcol[None, :] >= qstart[:, None]) & (kcol[None, :] < qstart[:, None] + NA_COLS)
    m = jnp.where(valid, toep, NEG_INF).astype(F32)
    return jnp.concatenate([m[:, :, :ndr - 1], m[:, :, 1:]], axis=-1)


def _span_tables(seq_lens):
    nspan = sum(seq_lens) // SPAN
    first = np.zeros(nspan, np.int32)
    last = np.zeros(nspan, np.int32)
    koff, case = [], []
    s0 = 0
    for L in seq_lens:
        ns = L // SPAN
        first[s0] = 1
        last[s0 + ns - 1] = 1
        s0 += ns
        rows = L // GRID_W
        r = np.arange(rows)
        delta = np.clip(r - NA_ROWS // 2, 0, rows - NA_ROWS) - r
        koff.append((delta + r % ROWS_PER_SPAN) * GRID_W + C_NBR)
        case.append(delta + NA_ROWS - 1)
    flags = jnp.asarray(np.stack([first, last]))
    tab_c = jnp.asarray(np.stack([np.concatenate(koff), np.concatenate(case)]).astype(np.int32))
    return flags, tab_c, nspan


def _qkv_norm_tables(qk_norm):
    segs = [(A_HEADS, 0), (A_HEADS, 1), (A_HEADS, None), (B_Q_HEADS, 2), (B_KV_HEADS, 3), (B_KV_HEADS, None),
            (C_HEADS, 4), (C_HEADS, 5), (C_HEADS, None)]
    gains, flags = [], []
    for nh, idx in segs:
        if idx is None:
            gains.append(jnp.ones((DEPTH, nh * HEAD_DIM), F32))
            flags.append(np.zeros(nh * HEAD_DIM, np.float32))
        else:
            gains.append(jnp.tile(qk_norm[:, idx, :].astype(F32), (1, nh)))
            flags.append(np.ones(nh * HEAD_DIM, np.float32))
    gain = jnp.concatenate(gains, axis=1)[:, None, :]
    flag = jnp.asarray(np.concatenate(flags))[None, :]
    return gain, flag


def kernel(x_prompt, x_sample, rel_bias, ffn1_norm, ffn1_w_in, ffn1_w_out, mix_norm, w_in, qk_norm, sink, rpb,
           w_branch_a, w_branch_b, w_branch_c, w_out, ffn2_norm, ffn2_w_in, ffn2_w_out):
    seq_lens = (x_prompt.shape[1],) * x_prompt.shape[0] + (x_sample.shape[1],) * x_sample.shape[0]
    assert all(L % SPAN == 0 for L in seq_lens)
    x = jnp.concatenate([x_prompt.reshape(-1, D_MODEL), x_sample.reshape(-1, D_MODEL)], axis=0)

    flags, tab_c, nspan = _span_tables(seq_lens)
    bias_a, bias_b = _bias_tables_ab(rel_bias.astype(F32))
    bias_c = _bias_table_c(rpb.astype(F32))
    qk_gain, qk_flag = _qkv_norm_tables(qk_norm)
    rep = B_Q_HEADS // B_KV_HEADS
    sink_rows = jnp.broadcast_to(
        jnp.repeat(sink.astype(F32), B_HALF_WINDOW, axis=1).reshape(DEPTH, B_KV_HEADS, rep * B_HALF_WINDOW, 1),
        (DEPTH, B_KV_HEADS, rep * B_HALF_WINDOW, HEAD_DIM))

    bf = lambda w: w.astype(BF16)
    f1_in, f1_out, f2_in, f2_out = bf(ffn1_w_in), bf(ffn1_w_out), bf(ffn2_w_in), bf(ffn2_w_out)
    w_in_b, w_ba, w_bb, w_bc, w_o = bf(w_in), bf(w_branch_a), bf(w_branch_b), bf(w_branch_c), bf(w_out)
    n1 = ffn1_norm.astype(F32)[:, None, :]
    nm = mix_norm.astype(F32)[:, None, :]
    n2 = ffn2_norm.astype(F32)[:, None, :]

    for l in range(DEPTH):
        x = _ffn(x, n1, f1_in, f1_out, l)
        qkv = _qkv_proj(x, nm, w_in_b, qk_gain, qk_flag, l)
        o_a = _attn_a(qkv, bias_a, flags, nspan)
        o_b = _attn_b(qkv, bias_b, sink_rows[l], flags, nspan)
        o_c = _attn_c(qkv, bias_c, tab_c, nspan, l)
        x = _mix_out(x, nm, o_a, o_b, o_c, w_in_b, w_ba, w_bb, w_bc, w_o, l)
        x = _ffn(x, n2, f2_in, f2_out, l)

    n_p = x_prompt.shape[0] * x_prompt.shape[1]
    return (x[:n_p].reshape(x_prompt.shape), x[n_p:].reshape(x_sample.shape))
```

```python
import functools

import numpy as np
import jax
import jax.numpy as jnp
from jax import lax
from jax.experimental import pallas as pl
from jax.experimental.pallas import tpu as pltpu

F32 = jnp.float32
BF16 = jnp.bfloat16

D_MODEL = 2048
DEPTH = 4
HEAD_DIM = 128
DILATED_GROUPS = ((128, 1), (512, 4), (2048, 16))
A_HEADS_PER_GROUP = 4
A_HEADS = 12
A_OUT = 512
B_Q_HEADS = 8
B_KV_HEADS = 2
B_HALF_WINDOW = 128
B_OUT = 1024
C_HEADS = 8
C_OUT = 1024
GRID_W = 64
NA_ROWS = 8
NA_COLS = 16
T5_BUCKETS = 32
T5_MAX_DIST = 1024
D_FF = 5632
EPS = 1e-6
NEG_INF = -1e30
SM_SCALE = HEAD_DIM ** -0.5
QKV_COLS = 9216
N_QKV_HEADS = QKV_COLS // HEAD_DIM
GATE_COL0 = QKV_COLS

SPAN = 1024
A_HW = 64
VMEM_LIMIT = 56 * 1024 * 1024


def _t5_bucket(rel):
    nb = T5_BUCKETS // 2
    max_exact = nb // 2
    sign = (rel > 0).astype(np.int32) * nb
    n = np.abs(rel)
    large = max_exact + (np.log(np.maximum(n, 1) / max_exact) / np.log(T5_MAX_DIST / max_exact)
                         * (nb - max_exact)).astype(np.int32)
    large = np.minimum(large, nb - 1)
    return sign + np.where(n < max_exact, n, large)


def _rms(x, g):
    ms = jnp.mean(x * x, axis=-1, keepdims=True)
    return (x * lax.rsqrt(ms + EPS)) * g


def _skewed(n, first, second, skew):
    pending = [first(i) for i in range(min(skew, n))]
    for i in range(n):
        cur = pending.pop(0)
        if i + skew < n:
            pending.append(first(i + skew))
        second(i, cur)


def _ffn_kernel(x_ref, g_ref, wg_ref, wu_ref, wo_ref, o_ref, h_ref, *, nf, nchunk):
    f = pl.program_id(1)
    rows_per = x_ref.shape[0] // nchunk

    def body(first, last):
        def up_proj(c):
            rows = slice(c * rows_per, (c + 1) * rows_per)
            if first:
                h = _rms(x_ref[rows, :], g_ref[...]).astype(BF16)
                h_ref[rows, :] = h
            else:
                h = h_ref[rows, :]
            gate = jnp.dot(h, wg_ref[...], preferred_element_type=F32)
            up = jnp.dot(h, wu_ref[...], preferred_element_type=F32)
            return gate, up

        def down_proj(c, gate_up):
            rows = slice(c * rows_per, (c + 1) * rows_per)
            gate, up = gate_up
            act = ((gate * jax.nn.sigmoid(gate)) * up).astype(BF16)
            acc = jnp.dot(act, wo_ref[...], preferred_element_type=F32)
            if not first:
                acc = o_ref[rows, :] + acc
            o_ref[rows, :] = (x_ref[rows, :] + 0.5 * acc) if last else acc

        _skewed(nchunk, up_proj, down_proj, skew=1)

    assert nf >= 2
    pl.when(f == 0)(lambda: body(True, False))
    pl.when(jnp.logical_and(f > 0, f < nf - 1))(lambda: body(False, False))
    pl.when(f == nf - 1)(lambda: body(False, True))


def _ffn(x, g, w_in, w_out, layer, *, tm=1024, tf=512, nchunk=4):
    t = x.shape[0]
    nf = D_FF // tf
    return pl.pallas_call(
        functools.partial(_ffn_kernel, nf=nf, nchunk=nchunk),
        grid=(t // tm, nf),
        in_specs=[
            pl.BlockSpec((tm, D_MODEL), lambda i, f: (i, 0)),
            pl.BlockSpec((None, 1, D_MODEL), lambda i, f: (layer, 0, 0)),
            pl.BlockSpec((None, D_MODEL, tf), lambda i, f: (layer, 0, f)),
            pl.BlockSpec((None, D_MODEL, tf), lambda i, f: (layer, 0, nf + f)),
            pl.BlockSpec((None, tf, D_MODEL), lambda i, f: (layer, f, 0)),
        ],
        out_specs=pl.BlockSpec((tm, D_MODEL), lambda i, f: (i, 0)),
        out_shape=jax.ShapeDtypeStruct((t, D_MODEL), F32),
        scratch_shapes=[pltpu.VMEM((tm, D_MODEL), BF16)],
        compiler_params=pltpu.CompilerParams(
            dimension_semantics=("parallel", "arbitrary"), vmem_limit_bytes=VMEM_LIMIT),
        name="ffn",
    )(x, g, w_in, w_in, w_out)


def _qkv_kernel(x_ref, g_ref, w_ref, gain_ref, flag_ref, o_ref, h_ref, stage_ref, *, hpb):
    n = pl.program_id(1)
    nchunk = 4
    rows_per = SPAN // nchunk
    quads = hpb // A_HEADS_PER_GROUP

    @pl.when(n == 0)
    def _():
        h_ref[...] = _rms(x_ref[...], g_ref[...]).astype(BF16)

    def project(c):
        return jnp.dot(h_ref[c * rows_per:(c + 1) * rows_per, :], w_ref[...], preferred_element_type=F32)

    def head_norm(c, acc):
        rows = slice(c * rows_per, (c + 1) * rows_per)
        for hh in range(hpb):
            sl = slice(hh * HEAD_DIM, (hh + 1) * HEAD_DIM)
            a = acc[:, sl]
            normed = _rms(a, gain_ref[:, sl])
            res = jnp.where(flag_ref[:, sl] > 0.0, normed, a)
            stage_ref[hh, rows, :] = res
            o_ref[hh, rows, :] = res.astype(BF16)

    _skewed(nchunk, project, head_norm, skew=1)

    for qd in range(quads):
        quad = n * quads + qd
        group = jnp.where(quad < 3 * len(DILATED_GROUPS), quad % 3, 0)
        for gi, (_, d) in enumerate(DILATED_GROUPS):
            if d == 1:
                continue

            @pl.when(group == gi)
            def _(d=d, qd=qd):
                per = SPAN // d
                for hh in range(qd * A_HEADS_PER_GROUP, (qd + 1) * A_HEADS_PER_GROUP):
                    for r in range(d):
                        o_ref[hh, r * per:(r + 1) * per, :] = (
                            stage_ref[hh, pl.ds(r, per, stride=d), :].astype(BF16))


def _qkv_proj(x, g, w_in, gain, flag, layer, *, tn=1024):
    t = x.shape[0]
    nn = QKV_COLS // tn
    hpb = tn // HEAD_DIM
    return pl.pallas_call(
        functools.partial(_qkv_kernel, hpb=hpb),
        grid=(t // SPAN, nn),
        in_specs=[
            pl.BlockSpec((SPAN, D_MODEL), lambda i, n: (i, 0)),
            pl.BlockSpec((None, 1, D_MODEL), lambda i, n: (layer, 0, 0)),
            pl.BlockSpec((None, D_MODEL, tn), lambda i, n: (layer, 0, n)),
            pl.BlockSpec((None, 1, tn), lambda i, n: (layer, 0, n)),
            pl.BlockSpec((1, tn), lambda i, n: (0, n)),
        ],
        out_specs=pl.BlockSpec((hpb, SPAN, HEAD_DIM), lambda i, n: (n, i, 0)),
        out_shape=jax.ShapeDtypeStruct((N_QKV_HEADS, t, HEAD_DIM), BF16),
        scratch_shapes=[pltpu.VMEM((SPAN, D_MODEL), BF16), pltpu.VMEM((hpb, SPAN, HEAD_DIM), F32)],
        compiler_params=pltpu.CompilerParams(
            dimension_semantics=("parallel", "arbitrary"), vmem_limit_bytes=VMEM_LIMIT),
        name="qkv_proj",
    )(x, g, w_in, gain, flag)


def _attn_a_kernel(flags_ref, *refs):
    q_refs = refs[0:3]
    k_refs = refs[3:12]
    v_refs = refs[12:21]
    bias_ref = refs[21]
    o_ref = refs[22]
    acc_ref, m_ref, l_ref = refs[23:26]

    s_idx = pl.program_id(0)
    is_first = flags_ref[0, s_idx] > 0
    is_last = flags_ref[1, s_idx] > 0
    lane = lax.broadcasted_iota(jnp.int32, (1, 3 * A_HW), 1)
    pen_first = jnp.where(jnp.logical_and(lane < A_HW, is_first), NEG_INF, 0.0).astype(F32)
    pen_last = jnp.where(jnp.logical_and(lane >= 2 * A_HW, is_last), NEG_INF, 0.0).astype(F32)
    nchunk = SPAN // A_HW

    def chunk_plan(idx):
        g, c = divmod(idx, nchunk)
        d = DILATED_GROUPS[g][1]
        cpr = nchunk // d
        r, jb = divmod(c, cpr)
        prev = (1, c - 1, False) if jb > 0 else (0, r * cpr + cpr - 1, True)
        nxt = (1, c + 1, False) if jb < cpr - 1 else (2, r * cpr, True)
        if d == 1:
            prev = prev if jb > 0 else (0, 0, True)
            nxt = nxt if jb < cpr - 1 else (2, 0, True)
        return g, d, c, r, jb, prev, nxt

    def cat3(refs, g, c, prev, nxt):
        def chunk(which, cc):
            return refs[3 * g + which][cc * A_HW:(cc + 1) * A_HW, :]
        return jnp.concatenate([chunk(prev[0], prev[1]), chunk(1, c), chunk(nxt[0], nxt[1])], axis=0)

    def scores(idx):
        g, d, c, r, jb, prev, nxt = chunk_plan(idx)
        q = q_refs[g][c * A_HW:(c + 1) * A_HW, :]
        kcat = cat3(k_refs, g, c, prev, nxt)
        s = lax.dot_general(q, kcat, (((1,), (1,)), ((), ())), preferred_element_type=F32)
        s = s * SM_SCALE + bias_ref[g]
        if prev[2]:
            s = s + pen_first
        if nxt[2]:
            s = s + pen_last
        return s

    def finish(idx, s):
        g, d, c, r, jb, prev, nxt = chunk_plan(idx)
        vcat = cat3(v_refs, g, c, prev, nxt)
        m = jnp.max(s, axis=-1, keepdims=True)
        p = jnp.exp(s - m)
        l = jnp.sum(p, axis=-1, keepdims=True)
        acc = jnp.dot(p.astype(BF16), vcat, preferred_element_type=F32)
        if d == 1:
            dst = slice(c * A_HW, (c + 1) * A_HW)
        else:
            dst = pl.ds(jb * A_HW * d + r, A_HW, stride=d)
        acc_ref[g, dst, :] = acc
        m_ref[g, dst, :] = jnp.broadcast_to(m, (A_HW, HEAD_DIM))
        l_ref[g, dst, :] = jnp.broadcast_to(l, (A_HW, HEAD_DIM))

    _skewed(len(DILATED_GROUPS) * nchunk, scores, finish, skew=48)

    m_all = jnp.maximum(jnp.maximum(m_ref[0], m_ref[1]), m_ref[2])
    w0 = jnp.exp(m_ref[0] - m_all)
    w1 = jnp.exp(m_ref[1] - m_all)
    w2 = jnp.exp(m_ref[2] - m_all)
    numer = w0 * acc_ref[0] + w1 * acc_ref[1] + w2 * acc_ref[2]
    denom = w0 * l_ref[0] + w1 * l_ref[1] + w2 * l_ref[2]
    o_ref[...] = (numer / denom).astype(BF16)


def _attn_a(qkv, bias, flags, nspan):
    blk = (None, SPAN, HEAD_DIM)

    def q_spec(g):
        return pl.BlockSpec(blk, lambda s, j, fl, g=g: (g * 4 + j, s, 0))

    nchunk = SPAN // A_HW
    chunk_blk = (None, A_HW, HEAD_DIM)

    def kv_specs(base, g):
        cur = pl.BlockSpec(blk, lambda s, j, fl, g=g: (base + g * 4 + j, s, 0))
        if DILATED_GROUPS[g][1] == 1:
            return [
                pl.BlockSpec(chunk_blk,
                             lambda s, j, fl, g=g: (base + g * 4 + j, jnp.maximum(s * nchunk - 1, 0), 0)),
                cur,
                pl.BlockSpec(chunk_blk, lambda s, j, fl, g=g: (
                    base + g * 4 + j, jnp.minimum((s + 1) * nchunk, nspan * nchunk - 1), 0)),
            ]
        return [
            pl.BlockSpec(blk, lambda s, j, fl, g=g: (base + g * 4 + j, jnp.maximum(s - 1, 0), 0)),
            cur,
            pl.BlockSpec(blk, lambda s, j, fl, g=g: (base + g * 4 + j, jnp.minimum(s + 1, nspan - 1), 0)),
        ]

    in_specs = [q_spec(g) for g in range(3)]
    for base in (A_HEADS, 2 * A_HEADS):
        for g in range(3):
            in_specs += kv_specs(base, g)
    in_specs.append(pl.BlockSpec((3, None, A_HW, 3 * A_HW), lambda s, j, fl: (0, j, 0, 0)))
    grid_spec = pltpu.PrefetchScalarGridSpec(
        num_scalar_prefetch=1,
        grid=(nspan, A_HEADS_PER_GROUP),
        in_specs=in_specs,
        out_specs=pl.BlockSpec((SPAN, HEAD_DIM), lambda s, j, fl: (s, j)),
        scratch_shapes=[pltpu.VMEM((3, SPAN, HEAD_DIM), F32)] * 3,
    )
    return pl.pallas_call(
        _attn_a_kernel,
        grid_spec=grid_spec,
        out_shape=jax.ShapeDtypeStruct((nspan * SPAN, A_OUT), BF16),
        compiler_params=pltpu.CompilerParams(
            dimension_semantics=("parallel", "arbitrary"), vmem_limit_bytes=VMEM_LIMIT),
        name="attn_dilated",
    )(flags, *([qkv] * 21), bias)


def _attn_b_kernel(flags_ref, q_ref, kp_ref, kc_ref, kn_ref, vp_ref, vc_ref, vn_ref, bias_ref, sink_ref, o_ref):
    hw = B_HALF_WINDOW
    rep = B_Q_HEADS // B_KV_HEADS
    s_idx = pl.program_id(0)
    is_first = flags_ref[0, s_idx] > 0
    is_last = flags_ref[1, s_idx] > 0
    lane = lax.broadcasted_iota(jnp.int32, (1, 3 * hw), 1)
    pen_first = jnp.where(jnp.logical_and(lane < hw, is_first), NEG_INF, 0.0).astype(F32)
    pen_last = jnp.where(jnp.logical_and(lane >= 2 * hw, is_last), NEG_INF, 0.0).astype(F32)
    kfull = jnp.concatenate([kp_ref[...], kc_ref[...], kn_ref[...]], axis=0)
    vfull = jnp.concatenate([vp_ref[...], vc_ref[...], vn_ref[...]], axis=0)
    nblk = SPAN // hw
    wide = (rep * hw, HEAD_DIM)

    def scores(b):
        q = q_ref[:, b * hw:(b + 1) * hw, :].reshape(rep * hw, HEAD_DIM)
        kcat = kfull[b * hw:(b + 3) * hw, :]
        s = lax.dot_general(q, kcat, (((1,), (1,)), ((), ())), preferred_element_type=F32)
        s = s * SM_SCALE + bias_ref[...]
        if b == 0:
            s = s + pen_first
        if b == nblk - 1:
            s = s + pen_last
        return s

    def finish(b, s):
        vcat = vfull[b * hw:(b + 3) * hw, :]
        m = jnp.max(s, axis=-1, keepdims=True)
        p = jnp.exp(s - m)
        den = jnp.broadcast_to(jnp.sum(p, axis=-1, keepdims=True), wide)
        num = jnp.dot(p.astype(BF16), vcat, preferred_element_type=F32)
        m = jnp.broadcast_to(m, wide)
        sink = sink_ref[...]
        m2 = jnp.maximum(m, sink)
        a = jnp.exp(m - m2)
        o = (num * a) / (den * a + jnp.exp(sink - m2))
        for hh in range(rep):
            o_ref[b * hw:(b + 1) * hw, hh * HEAD_DIM:(hh + 1) * HEAD_DIM] = (
                o[hh * hw:(hh + 1) * hw, :].astype(BF16))

    _skewed(nblk, scores, finish, skew=2)


def _attn_b(qkv, bias, sink_rows, flags, nspan):
    hw = B_HALF_WINDOW
    rep = B_Q_HEADS // B_KV_HEADS
    q0 = 3 * A_HEADS
    k0 = q0 + B_Q_HEADS
    v0 = k0 + B_KV_HEADS
    per = SPAN // hw
    nb_total = nspan * per

    def nbr_specs(base):
        return [
            pl.BlockSpec((None, hw, HEAD_DIM), lambda s, g, fl: (base + g, jnp.maximum(s * per - 1, 0), 0)),
            pl.BlockSpec((None, SPAN, HEAD_DIM), lambda s, g, fl: (base + g, s, 0)),
            pl.BlockSpec((None, hw, HEAD_DIM),
                         lambda s, g, fl: (base + g, jnp.minimum(s * per + per, nb_total - 1), 0)),
        ]

    in_specs = [pl.BlockSpec((rep, SPAN, HEAD_DIM), lambda s, g, fl: (q0 // rep + g, s, 0))]
    in_specs += nbr_specs(k0) + nbr_specs(v0)
    in_specs.append(pl.BlockSpec((None, rep * hw, 3 * hw), lambda s, g, fl: (g, 0, 0)))
    in_specs.append(pl.BlockSpec((None, rep * hw, HEAD_DIM), lambda s, g, fl: (g, 0, 0)))
    grid_spec = pltpu.PrefetchScalarGridSpec(
        num_scalar_prefetch=1,
        grid=(nspan, B_KV_HEADS),
        in_specs=in_specs,
        out_specs=pl.BlockSpec((SPAN, rep * HEAD_DIM), lambda s, g, fl: (s, g)),
    )
    return pl.pallas_call(
        _attn_b_kernel,
        grid_spec=grid_spec,
        out_shape=jax.ShapeDtypeStruct((nspan * SPAN, B_OUT), BF16),
        compiler_params=pltpu.CompilerParams(
            dimension_semantics=("parallel", "arbitrary"), vmem_limit_bytes=VMEM_LIMIT),
        name="attn_window",
    )(flags, qkv, qkv, qkv, qkv, qkv, qkv, qkv, bias, sink_rows)


C_KEYS = NA_ROWS * GRID_W
C_NBR = 256
ROWS_PER_SPAN = SPAN // GRID_W


def _attn_c_kernel(tab_ref, q_ref, kp_ref, kc_ref, kn_ref, vp_ref, vc_ref, vn_ref, bias_ref, o_ref,
                   kfull_ref, vfull_ref):
    s_idx = pl.program_id(0)
    kfull_ref[:, 0:C_NBR, :] = kp_ref[...]
    kfull_ref[:, C_NBR:C_NBR + SPAN, :] = kc_ref[...]
    kfull_ref[:, C_NBR + SPAN:, :] = kn_ref[...]
    vfull_ref[:, 0:C_NBR, :] = vp_ref[...]
    vfull_ref[:, C_NBR:C_NBR + SPAN, :] = vc_ref[...]
    vfull_ref[:, C_NBR + SPAN:, :] = vn_ref[...]

    def scores(idx):
        hh, rr = divmod(idx, ROWS_PER_SPAN)
        row = s_idx * ROWS_PER_SPAN + rr
        koff = pl.multiple_of(tab_ref[0, row], GRID_W)
        q = q_ref[hh, rr * GRID_W:(rr + 1) * GRID_W, :]
        kcat = kfull_ref[hh, pl.ds(koff, C_KEYS), :]
        s = lax.dot_general(q, kcat, (((1,), (1,)), ((), ())), preferred_element_type=F32)
        di = tab_ref[1, row]
        bias = jnp.concatenate([bias_ref[hh, di + 2 * kk] for kk in range(NA_ROWS // 2)], axis=1)
        return s * SM_SCALE + bias

    def finish(idx, s):
        hh, rr = divmod(idx, ROWS_PER_SPAN)
        row = s_idx * ROWS_PER_SPAN + rr
        koff = pl.multiple_of(tab_ref[0, row], GRID_W)
        vcat = vfull_ref[hh, pl.ds(koff, C_KEYS), :]
        m = jnp.max(s, axis=-1, keepdims=True)
        p = jnp.exp(s - m)
        l = jnp.sum(p, axis=-1, keepdims=True)
        p = p / l
        o = jnp.dot(p.astype(BF16), vcat, preferred_element_type=F32)
        o_ref[rr * GRID_W:(rr + 1) * GRID_W, hh * HEAD_DIM:(hh + 1) * HEAD_DIM] = o.astype(BF16)

    _skewed(q_ref.shape[0] * ROWS_PER_SPAN, scores, finish, skew=64)


def _attn_c(qkv, bias, tab, nspan, layer, *, hps=4):
    q0 = 3 * A_HEADS + B_Q_HEADS + 2 * B_KV_HEADS
    k0 = q0 + C_HEADS
    v0 = k0 + C_HEADS
    per = SPAN // C_NBR
    nb_total = nspan * per
    assert q0 % hps == 0 and C_HEADS % hps == 0

    def nbr_specs(base):
        return [
            pl.BlockSpec((hps, C_NBR, HEAD_DIM),
                         lambda s, h, tb: (base // hps + h, jnp.maximum(s * per - 1, 0), 0)),
            pl.BlockSpec((hps, SPAN, HEAD_DIM), lambda s, h, tb: (base // hps + h, s, 0)),
            pl.BlockSpec((hps, C_NBR, HEAD_DIM),
                         lambda s, h, tb: (base // hps + h, jnp.minimum(s * per + per, nb_total - 1), 0)),
        ]

    in_specs = [pl.BlockSpec((hps, SPAN, HEAD_DIM), lambda s, h, tb: (q0 // hps + h, s, 0))]
    in_specs += nbr_specs(k0) + nbr_specs(v0)
    in_specs.append(pl.BlockSpec((None, hps, 2 * NA_ROWS - 2, GRID_W, 2 * GRID_W),
                                 lambda s, h, tb: (layer, h, 0, 0, 0)))
    grid_spec = pltpu.PrefetchScalarGridSpec(
        num_scalar_prefetch=1,
        grid=(nspan, C_HEADS // hps),
        in_specs=in_specs,
        out_specs=pl.BlockSpec((SPAN, hps * HEAD_DIM), lambda s, h, tb: (s, h)),
        scratch_shapes=[pltpu.VMEM((hps, SPAN + 2 * C_NBR, HEAD_DIM), BF16)] * 2,
    )
    return pl.pallas_call(
        _attn_c_kernel,
        grid_spec=grid_spec,
        out_shape=jax.ShapeDtypeStruct((nspan * SPAN, C_OUT), BF16),
        compiler_params=pltpu.CompilerParams(
            dimension_semantics=("parallel", "arbitrary"), vmem_limit_bytes=VMEM_LIMIT),
        name="attn_nbr",
    )(tab, qkv, qkv, qkv, qkv, qkv, qkv, qkv, bias)


def _mix_kernel(x_ref, g_ref, oa_ref, ob_ref, oc_ref, wg0_ref, wg1_ref, wg2_ref, wba_ref, wbb_ref, wbc_ref,
                wo_ref, y_ref, h_ref, *, nc, nchunk):
    c = pl.program_id(1)
    rows_per = x_ref.shape[0] // nchunk

    def body(first, last):
        def branch_dots(ci):
            rows = slice(ci * rows_per, (ci + 1) * rows_per)
            if first:
                h = _rms(x_ref[rows, :], g_ref[...]).astype(BF16)
                h_ref[rows, :] = h
            else:
                h = h_ref[rows, :]
            g0 = jnp.dot(h, wg0_ref[...], preferred_element_type=F32)
            g1 = jnp.dot(h, wg1_ref[...], preferred_element_type=F32)
            g2 = jnp.dot(h, wg2_ref[...], preferred_element_type=F32)
            pa = jnp.dot(oa_ref[rows, :], wba_ref[...], preferred_element_type=F32)
            pb = jnp.dot(ob_ref[rows, :], wbb_ref[...], preferred_element_type=F32)
            pc = jnp.dot(oc_ref[rows, :], wbc_ref[...], preferred_element_type=F32)
            return g0, g1, g2, pa, pb, pc

        def mix(ci, vals):
            rows = slice(ci * rows_per, (ci + 1) * rows_per)
            g0, g1, g2, pa, pb, pc = vals
            z = jax.nn.sigmoid(g0) * pa + jax.nn.sigmoid(g1) * pb + jax.nn.sigmoid(g2) * pc
            acc = jnp.dot(z.astype(BF16), wo_ref[...], preferred_element_type=F32)
            if not first:
                acc = y_ref[rows, :] + acc
            y_ref[rows, :] = (x_ref[rows, :] + acc) if last else acc

        _skewed(nchunk, branch_dots, mix, skew=1)

    assert nc >= 2
    pl.when(c == 0)(lambda: body(True, False))
    pl.when(jnp.logical_and(c > 0, c < nc - 1))(lambda: body(False, False))
    pl.when(c == nc - 1)(lambda: body(False, True))


def _mix_out(x, g, o_a, o_b, o_c, w_in, w_ba, w_bb, w_bc, w_o, layer, *, tm=512, tn=512, nchunk=2):
    t = x.shape[0]
    nc = D_MODEL // tn
    gate_blk = GATE_COL0 // tn

    def gate_spec(b):
        return pl.BlockSpec((None, D_MODEL, tn), lambda i, c, b=b: (layer, 0, gate_blk + b * nc + c))

    return pl.pallas_call(
        functools.partial(_mix_kernel, nc=nc, nchunk=nchunk),
        grid=(t // tm, nc),
        in_specs=[
            pl.BlockSpec((tm, D_MODEL), lambda i, c: (i, 0)),
            pl.BlockSpec((None, 1, D_MODEL), lambda i, c: (layer, 0, 0)),
            pl.BlockSpec((tm, A_OUT), lambda i, c: (i, 0)),
            pl.BlockSpec((tm, B_OUT), lambda i, c: (i, 0)),
            pl.BlockSpec((tm, C_OUT), lambda i, c: (i, 0)),
            gate_spec(0), gate_spec(1), gate_spec(2),
            pl.BlockSpec((None, A_OUT, tn), lambda i, c: (layer, 0, c)),
            pl.BlockSpec((None, B_OUT, tn), lambda i, c: (layer, 0, c)),
            pl.BlockSpec((None, C_OUT, tn), lambda i, c: (layer, 0, c)),
            pl.BlockSpec((None, tn, D_MODEL), lambda i, c: (layer, c, 0)),
        ],
        out_specs=pl.BlockSpec((tm, D_MODEL), lambda i, c: (i, 0)),
        out_shape=jax.ShapeDtypeStruct((t, D_MODEL), F32),
        scratch_shapes=[pltpu.VMEM((tm, D_MODEL), BF16)],
        compiler_params=pltpu.CompilerParams(
            dimension_semantics=("parallel", "arbitrary"), vmem_limit_bytes=VMEM_LIMIT),
        name="mix_out",
    )(x, g, o_a, o_b, o_c, w_in, w_in, w_in, w_ba, w_bb, w_bc, w_o)


def _toeplitz(e, nrows, ncols, shift):
    period = e.shape[-1]
    assert nrows - 1 <= shift and shift + ncols <= period - 1
    flat = jnp.tile(e, (1,) * (e.ndim - 1) + (nrows,))[..., :nrows * (period - 1)]
    return flat.reshape(e.shape[:-1] + (nrows, period - 1))[..., shift:shift + ncols]


def _band_bias(rel_bias, hw, d, heads):
    u = np.arange(4 * hw)
    off = u - (2 * hw - 1)
    e = rel_bias[_t5_bucket(off * d)][:, heads].T
    e = jnp.where(np.abs(off)[None, :] <= hw, e, NEG_INF).astype(F32)
    return _toeplitz(e, hw, 3 * hw, hw - 1)


def _bias_tables_ab(rel_bias):
    bias_a = jnp.stack([
        _band_bias(rel_bias, w // (2 * d), d, slice(g * A_HEADS_PER_GROUP, (g + 1) * A_HEADS_PER_GROUP))
        for g, (w, d) in enumerate(DILATED_GROUPS)])
    hw = B_HALF_WINDOW
    rep = B_Q_HEADS // B_KV_HEADS
    bias_b = _band_bias(rel_bias, hw, 1, slice(A_HEADS, A_HEADS + B_Q_HEADS)).reshape(
        B_KV_HEADS, rep * hw, 3 * hw)
    return bias_a, bias_b


def _bias_table_c(rpb):
    ndr = 2 * NA_ROWS - 1
    ext = GRID_W - NA_COLS
    lead = jnp.broadcast_to(rpb[..., :1], rpb.shape[:-1] + (ext,))
    tail = jnp.broadcast_to(rpb[..., -1:], rpb.shape[:-1] + (ext + 1,))
    e = jnp.concatenate([lead, rpb, tail], axis=-1)
    toep = _toeplitz(e, GRID_W, GRID_W, GRID_W - 1)
    qcol = np.arange(GRID_W)
    kcol = np.arange(GRID_W)
    qstart = np.clip(qcol - NA_COLS // 2, 0, GRID_W - NA_COLS)
    valid = (kcol[None, :] >= qstart[:, None]) & (kcol[None, :] < qstart[:, None] + NA_COLS)
    m = jnp.where(valid, toep, NEG_INF).astype(F32)
    return jnp.concatenate([m[:, :, :ndr - 1], m[:, :, 1:]], axis=-1)


def _span_tables(seq_lens):
    nspan = sum(seq_lens) // SPAN
    first = np.zeros(nspan, np.int32)
    last = np.zeros(nspan, np.int32)
    koff, case = [], []
    s0 = 0
    for L in seq_lens:
        ns = L // SPAN
        first[s0] = 1
        last[s0 + ns - 1] = 1
        s0 += ns
        rows = L // GRID_W
        r = np.arange(rows)
        delta = np.clip(r - NA_ROWS // 2, 0, rows - NA_ROWS) - r
        koff.append((delta + r % ROWS_PER_SPAN) * GRID_W + C_NBR)
        case.append(delta + NA_ROWS - 1)
    flags = jnp.asarray(np.stack([first, last]))
    tab_c = jnp.asarray(np.stack([np.concatenate(koff), np.concatenate(case)]).astype(np.int32))
    return flags, tab_c, nspan


def _qkv_norm_tables(qk_norm):
    segs = [(A_HEADS, 0), (A_HEADS, 1), (A_HEADS, None), (B_Q_HEADS, 2), (B_KV_HEADS, 3), (B_KV_HEADS, None),
            (C_HEADS, 4), (C_HEADS, 5), (C_HEADS, None)]
    gains, flags = [], []
    for nh, idx in segs:
        if idx is None:
            gains.append(jnp.ones((DEPTH, nh * HEAD_DIM), F32))
            flags.append(np.zeros(nh * HEAD_DIM, np.float32))
        else:
            gains.append(jnp.tile(qk_norm[:, idx, :].astype(F32), (1, nh)))
            flags.append(np.ones(nh * HEAD_DIM, np.float32))
    gain = jnp.concatenate(gains, axis=1)[:, None, :]
    flag = jnp.asarray(np.concatenate(flags))[None, :]
    return gain, flag


def kernel(x_prompt, x_sample, rel_bias, ffn1_norm, ffn1_w_in, ffn1_w_out, mix_norm, w_in, qk_norm, sink, rpb,
           w_branch_a, w_branch_b, w_branch_c, w_out, ffn2_norm, ffn2_w_in, ffn2_w_out):
    seq_lens = (x_prompt.shape[1],) * x_prompt.shape[0] + (x_sample.shape[1],) * x_sample.shape[0]
    assert all(L % SPAN == 0 for L in seq_lens)
    x = jnp.concatenate([x_prompt.reshape(-1, D_MODEL), x_sample.reshape(-1, D_MODEL)], axis=0)

    flags, tab_c, nspan = _span_tables(seq_lens)
    bias_a, bias_b = _bias_tables_ab(rel_bias.astype(F32))
    bias_c = _bias_table_c(rpb.astype(F32))
    qk_gain, qk_flag = _qkv_norm_tables(qk_norm)
    rep = B_Q_HEADS // B_KV_HEADS
    sink_rows = jnp.broadcast_to(
        jnp.repeat(sink.astype(F32), B_HALF_WINDOW, axis=1).reshape(DEPTH, B_KV_HEADS, rep * B_HALF_WINDOW, 1),
        (DEPTH, B_KV_HEADS, rep * B_HALF_WINDOW, HEAD_DIM))

    bf = lambda w: w.astype(BF16)
    f1_in, f1_out, f2_in, f2_out = bf(ffn1_w_in), bf(ffn1_w_out), bf(ffn2_w_in), bf(ffn2_w_out)
    w_in_b, w_ba, w_bb, w_bc, w_o = bf(w_in), bf(w_branch_a), bf(w_branch_b), bf(w_branch_c), bf(w_out)
    n1 = ffn1_norm.astype(F32)[:, None, :]
    nm = mix_norm.astype(F32)[:, None, :]
    n2 = ffn2_norm.astype(F32)[:, None, :]

    for l in range(DEPTH):
        x = _ffn(x, n1, f1_in, f1_out, l)
        qkv = _qkv_proj(x, nm, w_in_b, qk_gain, qk_flag, l)
        o_a = _attn_a(qkv, bias_a, flags, nspan)
        o_b = _attn_b(qkv, bias_b, sink_rows[l], flags, nspan)
        o_c = _attn_c(qkv, bias_c, tab_c, nspan, l)
        x = _mix_out(x, nm, o_a, o_b, o_c, w_in_b, w_ba, w_bb, w_bc, w_o, l)
        x = _ffn(x, n2, f2_in, f2_out, l)

    n_p = x_prompt.shape[0] * x_prompt.shape[1]
    return (x[:n_p].reshape(x_prompt.shape), x[n_p:].reshape(x_sample.shape))
```

```python
import functools

import numpy as np
import jax
import jax.numpy as jnp
from jax import lax
from jax.experimental import pallas as pl
from jax.experimental.pallas import tpu as pltpu

F32 = jnp.float32
BF16 = jnp.bfloat16

D_MODEL = 2048
DEPTH = 4
HEAD_DIM = 128
DILATED_GROUPS = ((128, 1), (512, 4), (2048, 16))
A_HEADS_PER_GROUP = 4
A_HEADS = 12
A_OUT = 512
B_Q_HEADS = 8
B_KV_HEADS = 2
B_HALF_WINDOW = 128
B_OUT = 1024
C_HEADS = 8
C_OUT = 1024
GRID_W = 64
NA_ROWS = 8
NA_COLS = 16
T5_BUCKETS = 32
T5_MAX_DIST = 1024
D_FF = 5632
EPS = 1e-6
NEG_INF = -1e30
SM_SCALE = HEAD_DIM ** -0.5
QKV_COLS = 9216
N_QKV_HEADS = QKV_COLS // HEAD_DIM
GATE_COL0 = QKV_COLS

SPAN = 1024
A_HW = 64
VMEM_LIMIT = 56 * 1024 * 1024


def _t5_bucket(rel):
    nb = T5_BUCKETS // 2
    max_exact = nb // 2
    sign = (rel > 0).astype(np.int32) * nb
    n = np.abs(rel)
    large = max_exact + (np.log(np.maximum(n, 1) / max_exact) / np.log(T5_MAX_DIST / max_exact)
                         * (nb - max_exact)).astype(np.int32)
    large = np.minimum(large, nb - 1)
    return sign + np.where(n < max_exact, n, large)


def _rms(x, g):
    ms = jnp.mean(x * x, axis=-1, keepdims=True)
    return (x * lax.rsqrt(ms + EPS)) * g


def _skewed(n, first, second, skew):
    pending = [first(i) for i in range(min(skew, n))]
    for i in range(n):
        cur = pending.pop(0)
        if i + skew < n:
            pending.append(first(i + skew))
        second(i, cur)


def _ffn_kernel(x_ref, g_ref, wg_ref, wu_ref, wo_ref, o_ref, h_ref, *, nf, nchunk):
    f = pl.program_id(1)
    rows_per = x_ref.shape[0] // nchunk

    def body(first, last):
        def up_proj(c):
            rows = slice(c * rows_per, (c + 1) * rows_per)
            if first:
                h = _rms(x_ref[rows, :], g_ref[...]).astype(BF16)
                h_ref[rows, :] = h
            else:
                h = h_ref[rows, :]
            gate = jnp.dot(h, wg_ref[...], preferred_element_type=F32)
            up = jnp.dot(h, wu_ref[...], preferred_element_type=F32)
            return gate, up

        def down_proj(c, gate_up):
            rows = slice(c * rows_per, (c + 1) * rows_per)
            gate, up = gate_up
            act = ((gate * jax.nn.sigmoid(gate)) * up).astype(BF16)
            acc = jnp.dot(act, wo_ref[...], preferred_element_type=F32)
            if not first:
                acc = o_ref[rows, :] + acc
            o_ref[rows, :] = (x_ref[rows, :] + 0.5 * acc) if last else acc

        _skewed(nchunk, up_proj, down_proj, skew=1)

    assert nf >= 2
    pl.when(f == 0)(lambda: body(True, False))
    pl.when(jnp.logical_and(f > 0, f < nf - 1))(lambda: body(False, False))
    pl.when(f == nf - 1)(lambda: body(False, True))


FFN_TM = 1024


def _ffn(x, g, w_in, w_out, layer, *, rows=None, tf=512, nchunk=4):
    tm = FFN_TM
    r0, r1 = rows if rows is not None else (0, x.shape[0])
    assert r0 % tm == 0 and r1 % tm == 0
    tile0, t = r0 // tm, r1 - r0
    nf = D_FF // tf
    return pl.pallas_call(
        functools.partial(_ffn_kernel, nf=nf, nchunk=nchunk),
        grid=(t // tm, nf),
        in_specs=[
            pl.BlockSpec((tm, D_MODEL), lambda i, f: (tile0 + i, 0)),
            pl.BlockSpec((None, 1, D_MODEL), lambda i, f: (layer, 0, 0)),
            pl.BlockSpec((D_MODEL, tf), lambda i, f: (0, f)),
            pl.BlockSpec((D_MODEL, tf), lambda i, f: (0, nf + f)),
            pl.BlockSpec((tf, D_MODEL), lambda i, f: (f, 0)),
        ],
        out_specs=pl.BlockSpec((tm, D_MODEL), lambda i, f: (i, 0)),
        out_shape=jax.ShapeDtypeStruct((t, D_MODEL), F32),
        scratch_shapes=[pltpu.VMEM((tm, D_MODEL), BF16)],
        compiler_params=pltpu.CompilerParams(
            dimension_semantics=("parallel", "arbitrary"), vmem_limit_bytes=VMEM_LIMIT),
        name="ffn",
    )(x, g, w_in, w_in, w_out)


class _CastJob:
    def __init__(self, stacked, layer, grid_blocks):
        self.stacked = stacked
        self.layer = layer
        _, rows, cols = stacked.shape
        self.nr, self.nc = grid_blocks
        assert rows % self.nr == 0 and cols % self.nc == 0
        self.block = (rows // self.nr, cols // self.nc)
        assert self.block[0] % 16 == 0 and (self.block[1] % 128 == 0)
        self.out_shape = jax.ShapeDtypeStruct((rows, cols), BF16)

    def _index(self, i, j):
        ci = jnp.minimum(j, self.nc - 1) if self.nc > 1 else 0
        return i, ci

    def in_spec(self):
        return pl.BlockSpec((None,) + self.block, lambda i, j: (self.layer,) + self._index(i, j))

    def out_spec(self):
        return pl.BlockSpec(self.block, lambda i, j: self._index(i, j))


def _run_casts(in_refs, out_refs):
    for src, dst in zip(in_refs, out_refs):
        dst[...] = src[...].astype(BF16)


def _qkv_kernel(x_ref, g_ref, w_ref, gain_ref, flag_ref, *rest, hpb, ncast):
    cast_in = rest[:ncast]
    o_ref = rest[ncast]
    cast_out = rest[ncast + 1:2 * ncast + 1]
    h_ref, stage_ref = rest[2 * ncast + 1:]
    n = pl.program_id(1)
    nchunk = 4
    rows_per = SPAN // nchunk
    quads = hpb // A_HEADS_PER_GROUP

    @pl.when(n == 0)
    def _():
        h_ref[...] = _rms(x_ref[...], g_ref[...]).astype(BF16)

    _run_casts(cast_in, cast_out)

    def project(c):
        return jnp.dot(h_ref[c * rows_per:(c + 1) * rows_per, :], w_ref[...], preferred_element_type=F32)

    def head_norm(c, acc):
        rows = slice(c * rows_per, (c + 1) * rows_per)
        for hh in range(hpb):
            sl = slice(hh * HEAD_DIM, (hh + 1) * HEAD_DIM)
            a = acc[:, sl]
            normed = _rms(a, gain_ref[:, sl])
            res = jnp.where(flag_ref[:, sl] > 0.0, normed, a)
            stage_ref[hh, rows, :] = res
            o_ref[hh, rows, :] = res.astype(BF16)

    _skewed(nchunk, project, head_norm, skew=1)

    for qd in range(quads):
        quad = n * quads + qd
        group = jnp.where(quad < 3 * len(DILATED_GROUPS), quad % 3, 0)
        for gi, (_, d) in enumerate(DILATED_GROUPS):
            if d == 1:
                continue

            @pl.when(group == gi)
            def _(d=d, qd=qd):
                per = SPAN // d
                for hh in range(qd * A_HEADS_PER_GROUP, (qd + 1) * A_HEADS_PER_GROUP):
                    for r in range(d):
                        o_ref[hh, r * per:(r + 1) * per, :] = (
                            stage_ref[hh, pl.ds(r, per, stride=d), :].astype(BF16))


def _qkv_proj(x, g, w_in, gain, flag, layer, casts, *, tn=1024):
    t = x.shape[0]
    nn = QKV_COLS // tn
    hpb = tn // HEAD_DIM
    outs = pl.pallas_call(
        functools.partial(_qkv_kernel, hpb=hpb, ncast=len(casts)),
        grid=(t // SPAN, nn),
        in_specs=[
            pl.BlockSpec((SPAN, D_MODEL), lambda i, n: (i, 0)),
            pl.BlockSpec((None, 1, D_MODEL), lambda i, n: (layer, 0, 0)),
            pl.BlockSpec((D_MODEL, tn), lambda i, n: (0, n)),
            pl.BlockSpec((None, 1, tn), lambda i, n: (layer, 0, n)),
            pl.BlockSpec((1, tn), lambda i, n: (0, n)),
        ] + [c.in_spec() for c in casts],
        out_specs=[pl.BlockSpec((hpb, SPAN, HEAD_DIM), lambda i, n: (n, i, 0))] + [c.out_spec() for c in casts],
        out_shape=[jax.ShapeDtypeStruct((N_QKV_HEADS, t, HEAD_DIM), BF16)] + [c.out_shape for c in casts],
        scratch_shapes=[pltpu.VMEM((SPAN, D_MODEL), BF16), pltpu.VMEM((hpb, SPAN, HEAD_DIM), F32)],
        compiler_params=pltpu.CompilerParams(
            dimension_semantics=("parallel", "arbitrary"), vmem_limit_bytes=VMEM_LIMIT),
        name="qkv_proj",
    )(x, g, w_in, gain, flag, *[c.stacked for c in casts])
    return outs[0], outs[1:]


def _attn_a_kernel(flags_ref, *refs):
    q_refs = refs[0:3]
    k_refs = refs[3:12]
    v_refs = refs[12:21]
    bias_ref = refs[21]
    o_ref = refs[22]
    acc_ref, m_ref, l_ref = refs[23:26]

    s_idx = pl.program_id(0)
    is_first = flags_ref[0, s_idx] > 0
    is_last = flags_ref[1, s_idx] > 0
    lane = lax.broadcasted_iota(jnp.int32, (1, 3 * A_HW), 1)
    pen_first = jnp.where(jnp.logical_and(lane < A_HW, is_first), NEG_INF, 0.0).astype(F32)
    pen_last = jnp.where(jnp.logical_and(lane >= 2 * A_HW, is_last), NEG_INF, 0.0).astype(F32)
    nchunk = SPAN // A_HW

    def chunk_plan(idx):
        g, c = divmod(idx, nchunk)
        d = DILATED_GROUPS[g][1]
        cpr = nchunk // d
        r, jb = divmod(c, cpr)
        prev = (1, c - 1, False) if jb > 0 else (0, r * cpr + cpr - 1, True)
        nxt = (1, c + 1, False) if jb < cpr - 1 else (2, r * cpr, True)
        if d == 1:
            prev = prev if jb > 0 else (0, 0, True)
            nxt = nxt if jb < cpr - 1 else (2, 0, True)
        return g, d, c, r, jb, prev, nxt

    def cat3(refs, g, c, prev, nxt):
        def chunk(which, cc):
            return refs[3 * g + which][cc * A_HW:(cc + 1) * A_HW, :]
        return jnp.concatenate([chunk(prev[0], prev[1]), chunk(1, c), chunk(nxt[0], nxt[1])], axis=0)

    def scores(idx):
        g, d, c, r, jb, prev, nxt = chunk_plan(idx)
        q = q_refs[g][c * A_HW:(c + 1) * A_HW, :]
        kcat = cat3(k_refs, g, c, prev, nxt)
        s = lax.dot_general(q, kcat, (((1,), (1,)), ((), ())), preferred_element_type=F32)
        s = s * SM_SCALE + bias_ref[g]
        if prev[2]:
            s = s + pen_first
        if nxt[2]:
            s = s + pen_last
        return s

    def finish(idx, s):
        g, d, c, r, jb, prev, nxt = chunk_plan(idx)
        vcat = cat3(v_refs, g, c, prev, nxt)
        m = jnp.max(s, axis=-1, keepdims=True)
        p = jnp.exp(s - m)
        l = jnp.sum(p, axis=-1, keepdims=True)
        acc = jnp.dot(p.astype(BF16), vcat, preferred_element_type=F32)
        if d == 1:
            dst = slice(c * A_HW, (c + 1) * A_HW)
        else:
            dst = pl.ds(jb * A_HW * d + r, A_HW, stride=d)
        acc_ref[g, dst, :] = acc
        m_ref[g, dst, :] = jnp.broadcast_to(m, (A_HW, HEAD_DIM))
        l_ref[g, dst, :] = jnp.broadcast_to(l, (A_HW, HEAD_DIM))

    _skewed(len(DILATED_GROUPS) * nchunk, scores, finish, skew=48)

    m_all = jnp.maximum(jnp.maximum(m_ref[0], m_ref[1]), m_ref[2])
    w0 = jnp.exp(m_ref[0] - m_all)
    w1 = jnp.exp(m_ref[1] - m_all)
    w2 = jnp.exp(m_ref[2] - m_all)
    numer = w0 * acc_ref[0] + w1 * acc_ref[1] + w2 * acc_ref[2]
    denom = w0 * l_ref[0] + w1 * l_ref[1] + w2 * l_ref[2]
    o_ref[...] = (numer / denom).astype(BF16)


def _attn_a(qkv, bias, flags, nspan):
    blk = (None, SPAN, HEAD_DIM)

    def q_spec(g):
        return pl.BlockSpec(blk, lambda s, j, fl, g=g: (g * 4 + j, s, 0))

    nchunk = SPAN // A_HW
    chunk_blk = (None, A_HW, HEAD_DIM)

    def kv_specs(base, g):
        cur = pl.BlockSpec(blk, lambda s, j, fl, g=g: (base + g * 4 + j, s, 0))
        if DILATED_GROUPS[g][1] == 1:
            return [
                pl.BlockSpec(chunk_blk,
                             lambda s, j, fl, g=g: (base + g * 4 + j, jnp.maximum(s * nchunk - 1, 0), 0)),
                cur,
                pl.BlockSpec(chunk_blk, lambda s, j, fl, g=g: (
                    base + g * 4 + j, jnp.minimum((s + 1) * nchunk, nspan * nchunk - 1), 0)),
            ]
        return [
            pl.BlockSpec(blk, lambda s, j, fl, g=g: (base + g * 4 + j, jnp.maximum(s - 1, 0), 0)),
            cur,
            pl.BlockSpec(blk, lambda s, j, fl, g=g: (base + g * 4 + j, jnp.minimum(s + 1, nspan - 1), 0)),
        ]

    in_specs = [q_spec(g) for g in range(3)]
    for base in (A_HEADS, 2 * A_HEADS):
        for g in range(3):
            in_specs += kv_specs(base, g)
    in_specs.append(pl.BlockSpec((3, None, A_HW, 3 * A_HW), lambda s, j, fl: (0, j, 0, 0)))
    grid_spec = pltpu.PrefetchScalarGridSpec(
        num_scalar_prefetch=1,
        grid=(nspan, A_HEADS_PER_GROUP),
        in_specs=in_specs,
        out_specs=pl.BlockSpec((SPAN, HEAD_DIM), lambda s, j, fl: (s, j)),
        scratch_shapes=[pltpu.VMEM((3, SPAN, HEAD_DIM), F32)] * 3,
    )
    return pl.pallas_call(
        _attn_a_kernel,
        grid_spec=grid_spec,
        out_shape=jax.ShapeDtypeStruct((nspan * SPAN, A_OUT), BF16),
        compiler_params=pltpu.CompilerParams(
            dimension_semantics=("parallel", "arbitrary"), vmem_limit_bytes=VMEM_LIMIT),
        name="attn_dilated",
    )(flags, *([qkv] * 21), bias)


def _attn_b_kernel(flags_ref, q_ref, kp_ref, kc_ref, kn_ref, vp_ref, vc_ref, vn_ref, bias_ref, sink_ref, o_ref):
    hw = B_HALF_WINDOW
    rep = B_Q_HEADS // B_KV_HEADS
    s_idx = pl.program_id(1)
    is_first = flags_ref[0, s_idx] > 0
    is_last = flags_ref[1, s_idx] > 0
    lane = lax.broadcasted_iota(jnp.int32, (1, 3 * hw), 1)
    pen_first = jnp.where(jnp.logical_and(lane < hw, is_first), NEG_INF, 0.0).astype(F32)
    pen_last = jnp.where(jnp.logical_and(lane >= 2 * hw, is_last), NEG_INF, 0.0).astype(F32)
    kfull = jnp.concatenate([kp_ref[...], kc_ref[...], kn_ref[...]], axis=0)
    vfull = jnp.concatenate([vp_ref[...], vc_ref[...], vn_ref[...]], axis=0)
    nblk = SPAN // hw
    wide = (rep * hw, HEAD_DIM)

    def scores(b):
        q = q_ref[:, b * hw:(b + 1) * hw, :].reshape(rep * hw, HEAD_DIM)
        kcat = kfull[b * hw:(b + 3) * hw, :]
        s = lax.dot_general(q, kcat, (((1,), (1,)), ((), ())), preferred_element_type=F32)
        s = s * SM_SCALE + bias_ref[...]
        if b == 0:
            s = s + pen_first
        if b == nblk - 1:
            s = s + pen_last
        return s

    def finish(b, s):
        vcat = vfull[b * hw:(b + 3) * hw, :]
        m = jnp.max(s, axis=-1, keepdims=True)
        p = jnp.exp(s - m)
        den = jnp.broadcast_to(jnp.sum(p, axis=-1, keepdims=True), wide)
        num = jnp.dot(p.astype(BF16), vcat, preferred_element_type=F32)
        m = jnp.broadcast_to(m, wide)
        sink = sink_ref[...]
        m2 = jnp.maximum(m, sink)
        a = jnp.exp(m - m2)
        o = (num * a) / (den * a + jnp.exp(sink - m2))
        for hh in range(rep):
            o_ref[b * hw:(b + 1) * hw, hh * HEAD_DIM:(hh + 1) * HEAD_DIM] = (
                o[hh * hw:(hh + 1) * hw, :].astype(BF16))

    _skewed(nblk, scores, finish, skew=2)


def _attn_b(qkv, bias, sink_rows, flags, nspan):
    hw = B_HALF_WINDOW
    rep = B_Q_HEADS // B_KV_HEADS
    q0 = 3 * A_HEADS
    k0 = q0 + B_Q_HEADS
    v0 = k0 + B_KV_HEADS
    per = SPAN // hw
    nb_total = nspan * per

    def nbr_specs(base):
        return [
            pl.BlockSpec((None, hw, HEAD_DIM), lambda g, s, fl: (base + g, jnp.maximum(s * per - 1, 0), 0)),
            pl.BlockSpec((None, SPAN, HEAD_DIM), lambda g, s, fl: (base + g, s, 0)),
            pl.BlockSpec((None, hw, HEAD_DIM),
                         lambda g, s, fl: (base + g, jnp.minimum(s * per + per, nb_total - 1), 0)),
        ]

    in_specs = [pl.BlockSpec((rep, SPAN, HEAD_DIM), lambda g, s, fl: (q0 // rep + g, s, 0))]
    in_specs += nbr_specs(k0) + nbr_specs(v0)
    in_specs.append(pl.BlockSpec((None, rep * hw, 3 * hw), lambda g, s, fl: (g, 0, 0)))
    in_specs.append(pl.BlockSpec((None, rep * hw, HEAD_DIM), lambda g, s, fl: (g, 0, 0)))
    grid_spec = pltpu.PrefetchScalarGridSpec(
        num_scalar_prefetch=1,
        grid=(B_KV_HEADS, nspan),
        in_specs=in_specs,
        out_specs=pl.BlockSpec((SPAN, rep * HEAD_DIM), lambda g, s, fl: (s, g)),
    )
    return pl.pallas_call(
        _attn_b_kernel,
        grid_spec=grid_spec,
        out_shape=jax.ShapeDtypeStruct((nspan * SPAN, B_OUT), BF16),
        compiler_params=pltpu.CompilerParams(
            dimension_semantics=("parallel", "arbitrary"), vmem_limit_bytes=VMEM_LIMIT),
        name="attn_window",
    )(flags, qkv, qkv, qkv, qkv, qkv, qkv, qkv, bias, sink_rows)


C_KEYS = NA_ROWS * GRID_W
C_NBR = 256
ROWS_PER_SPAN = SPAN // GRID_W


def _attn_c_kernel(tab_ref, q_ref, kp_ref, kc_ref, kn_ref, vp_ref, vc_ref, vn_ref, bias_ref, o_ref,
                   kfull_ref, vfull_ref):
    s_idx = pl.program_id(1)
    kfull_ref[:, 0:C_NBR, :] = kp_ref[...]
    kfull_ref[:, C_NBR:C_NBR + SPAN, :] = kc_ref[...]
    kfull_ref[:, C_NBR + SPAN:, :] = kn_ref[...]
    vfull_ref[:, 0:C_NBR, :] = vp_ref[...]
    vfull_ref[:, C_NBR:C_NBR + SPAN, :] = vc_ref[...]
    vfull_ref[:, C_NBR + SPAN:, :] = vn_ref[...]

    def scores(idx):
        hh, rr = divmod(idx, ROWS_PER_SPAN)
        row = s_idx * ROWS_PER_SPAN + rr
        koff = pl.multiple_of(tab_ref[0, row], GRID_W)
        q = q_ref[hh, rr * GRID_W:(rr + 1) * GRID_W, :]
        kcat = kfull_ref[hh, pl.ds(koff, C_KEYS), :]
        s = lax.dot_general(q, kcat, (((1,), (1,)), ((), ())), preferred_element_type=F32)
        di = tab_ref[1, row]
        bias = jnp.concatenate([bias_ref[hh, di + 2 * kk] for kk in range(NA_ROWS // 2)], axis=1)
        return s * SM_SCALE + bias

    def finish(idx, s):
        hh, rr = divmod(idx, ROWS_PER_SPAN)
        row = s_idx * ROWS_PER_SPAN + rr
        koff = pl.multiple_of(tab_ref[0, row], GRID_W)
        vcat = vfull_ref[hh, pl.ds(koff, C_KEYS), :]
        m = jnp.max(s, axis=-1, keepdims=True)
        p = jnp.exp(s - m)
        l = jnp.sum(p, axis=-1, keepdims=True)
        p = p / l
        o = jnp.dot(p.astype(BF16), vcat, preferred_element_type=F32)
        o_ref[rr * GRID_W:(rr + 1) * GRID_W, hh * HEAD_DIM:(hh + 1) * HEAD_DIM] = o.astype(BF16)

    _skewed(q_ref.shape[0] * ROWS_PER_SPAN, scores, finish, skew=64)


def _attn_c(qkv, bias, tab, nspan, layer, *, hps=4):
    q0 = 3 * A_HEADS + B_Q_HEADS + 2 * B_KV_HEADS
    k0 = q0 + C_HEADS
    v0 = k0 + C_HEADS
    per = SPAN // C_NBR
    nb_total = nspan * per
    assert q0 % hps == 0 and C_HEADS % hps == 0

    def nbr_specs(base):
        return [
            pl.BlockSpec((hps, C_NBR, HEAD_DIM),
                         lambda h, s, tb: (base // hps + h, jnp.maximum(s * per - 1, 0), 0)),
            pl.BlockSpec((hps, SPAN, HEAD_DIM), lambda h, s, tb: (base // hps + h, s, 0)),
            pl.BlockSpec((hps, C_NBR, HEAD_DIM),
                         lambda h, s, tb: (base // hps + h, jnp.minimum(s * per + per, nb_total - 1), 0)),
        ]

    in_specs = [pl.BlockSpec((hps, SPAN, HEAD_DIM), lambda h, s, tb: (q0 // hps + h, s, 0))]
    in_specs += nbr_specs(k0) + nbr_specs(v0)
    in_specs.append(pl.BlockSpec((None, hps, 2 * NA_ROWS - 2, GRID_W, 2 * GRID_W),
                                 lambda h, s, tb: (layer, h, 0, 0, 0)))
    grid_spec = pltpu.PrefetchScalarGridSpec(
        num_scalar_prefetch=1,
        grid=(C_HEADS // hps, nspan),
        in_specs=in_specs,
        out_specs=pl.BlockSpec((SPAN, hps * HEAD_DIM), lambda h, s, tb: (s, h)),
        scratch_shapes=[pltpu.VMEM((hps, SPAN + 2 * C_NBR, HEAD_DIM), BF16)] * 2,
    )
    return pl.pallas_call(
        _attn_c_kernel,
        grid_spec=grid_spec,
        out_shape=jax.ShapeDtypeStruct((nspan * SPAN, C_OUT), BF16),
        compiler_params=pltpu.CompilerParams(
            dimension_semantics=("parallel", "arbitrary"), vmem_limit_bytes=VMEM_LIMIT),
        name="attn_nbr",
    )(tab, qkv, qkv, qkv, qkv, qkv, qkv, qkv, bias)


def _mix_kernel(x_ref, g_ref, oa_ref, ob_ref, oc_ref, wg0_ref, wg1_ref, wg2_ref, wba_ref, wbb_ref, wbc_ref,
                wo_ref, *rest, nc, nchunk, ncast):
    cast_in = rest[:ncast]
    y_ref = rest[ncast]
    cast_out = rest[ncast + 1:2 * ncast + 1]
    h_ref = rest[2 * ncast + 1]
    c = pl.program_id(1)
    rows_per = x_ref.shape[0] // nchunk

    def body(first, last):
        _run_casts(cast_in, cast_out)

        def branch_dots(ci):
            rows = slice(ci * rows_per, (ci + 1) * rows_per)
            if first:
                h = _rms(x_ref[rows, :], g_ref[...]).astype(BF16)
                h_ref[rows, :] = h
            else:
                h = h_ref[rows, :]
            g0 = jnp.dot(h, wg0_ref[...], preferred_element_type=F32)
            g1 = jnp.dot(h, wg1_ref[...], preferred_element_type=F32)
            g2 = jnp.dot(h, wg2_ref[...], preferred_element_type=F32)
            pa = jnp.dot(oa_ref[rows, :], wba_ref[...], preferred_element_type=F32)
            pb = jnp.dot(ob_ref[rows, :], wbb_ref[...], preferred_element_type=F32)
            pc = jnp.dot(oc_ref[rows, :], wbc_ref[...], preferred_element_type=F32)
            return g0, g1, g2, pa, pb, pc

        def mix(ci, vals):
            rows = slice(ci * rows_per, (ci + 1) * rows_per)
            g0, g1, g2, pa, pb, pc = vals
            z = jax.nn.sigmoid(g0) * pa + jax.nn.sigmoid(g1) * pb + jax.nn.sigmoid(g2) * pc
            acc = jnp.dot(z.astype(BF16), wo_ref[...], preferred_element_type=F32)
            if not first:
                acc = y_ref[rows, :] + acc
            y_ref[rows, :] = (x_ref[rows, :] + acc) if last else acc

        _skewed(nchunk, branch_dots, mix, skew=1)

    assert nc >= 2
    pl.when(c == 0)(lambda: body(True, False))
    pl.when(jnp.logical_and(c > 0, c < nc - 1))(lambda: body(False, False))
    pl.when(c == nc - 1)(lambda: body(False, True))


MIX_TM = 512
MIX_TN = 512


def _mix_out(x, g, o_a, o_b, o_c, w_in, w_ba, w_bb, w_bc, w_o, layer, casts, *, nchunk=2):
    t = x.shape[0]
    tm, tn = MIX_TM, MIX_TN
    nc = D_MODEL // tn
    gate_blk = GATE_COL0 // tn

    def gate_spec(b):
        return pl.BlockSpec((D_MODEL, tn), lambda i, c, b=b: (0, gate_blk + b * nc + c))

    outs = pl.pallas_call(
        functools.partial(_mix_kernel, nc=nc, nchunk=nchunk, ncast=len(casts)),
        grid=(t // tm, nc),
        in_specs=[
            pl.BlockSpec((tm, D_MODEL), lambda i, c: (i, 0)),
            pl.BlockSpec((None, 1, D_MODEL), lambda i, c: (layer, 0, 0)),
            pl.BlockSpec((tm, A_OUT), lambda i, c: (i, 0)),
            pl.BlockSpec((tm, B_OUT), lambda i, c: (i, 0)),
            pl.BlockSpec((tm, C_OUT), lambda i, c: (i, 0)),
            gate_spec(0), gate_spec(1), gate_spec(2),
            pl.BlockSpec((A_OUT, tn), lambda i, c: (0, c)),
            pl.BlockSpec((B_OUT, tn), lambda i, c: (0, c)),
            pl.BlockSpec((C_OUT, tn), lambda i, c: (0, c)),
            pl.BlockSpec((tn, D_MODEL), lambda i, c: (c, 0)),
        ] + [cj.in_spec() for cj in casts],
        out_specs=[pl.BlockSpec((tm, D_MODEL), lambda i, c: (i, 0))] + [cj.out_spec() for cj in casts],
        out_shape=[jax.ShapeDtypeStruct((t, D_MODEL), F32)] + [cj.out_shape for cj in casts],
        scratch_shapes=[pltpu.VMEM((tm, D_MODEL), BF16)],
        compiler_params=pltpu.CompilerParams(
            dimension_semantics=("parallel", "arbitrary"), vmem_limit_bytes=VMEM_LIMIT),
        name="mix_out",
    )(x, g, o_a, o_b, o_c, w_in, w_in, w_in, w_ba, w_bb, w_bc, w_o, *[cj.stacked for cj in casts])
    return outs[0], outs[1:]


def _toeplitz(e, nrows, ncols, shift):
    period = e.shape[-1]
    assert nrows - 1 <= shift and shift + ncols <= period - 1
    flat = jnp.tile(e, (1,) * (e.ndim - 1) + (nrows,))[..., :nrows * (period - 1)]
    return flat.reshape(e.shape[:-1] + (nrows, period - 1))[..., shift:shift + ncols]


def _band_bias(rel_bias, hw, d, heads):
    u = np.arange(4 * hw)
    off = u - (2 * hw - 1)
    e = rel_bias[_t5_bucket(off * d)][:, heads].T
    e = jnp.where(np.abs(off)[None, :] <= hw, e, NEG_INF).astype(F32)
    return _toeplitz(e, hw, 3 * hw, hw - 1)


def _bias_tables_ab(rel_bias):
    bias_a = jnp.stack([
        _band_bias(rel_bias, w // (2 * d), d, slice(g * A_HEADS_PER_GROUP, (g + 1) * A_HEADS_PER_GROUP))
        for g, (w, d) in enumerate(DILATED_GROUPS)])
    hw = B_HALF_WINDOW
    rep = B_Q_HEADS // B_KV_HEADS
    bias_b = _band_bias(rel_bias, hw, 1, slice(A_HEADS, A_HEADS + B_Q_HEADS)).reshape(
        B_KV_HEADS, rep * hw, 3 * hw)
    return bias_a, bias_b


def _bias_table_c(rpb):
    ndr = 2 * NA_ROWS - 1
    ext = GRID_W - NA_COLS
    lead = jnp.broadcast_to(rpb[..., :1], rpb.shape[:-1] + (ext,))
    tail = jnp.broadcast_to(rpb[..., -1:], rpb.shape[:-1] + (ext + 1,))
    e = jnp.concatenate([lead, rpb, tail], axis=-1)
    toep = _toeplitz(e, GRID_W, GRID_W, GRID_W - 1)
    qcol = np.arange(GRID_W)
    kcol = np.arange(GRID_W)
    qstart = np.clip(qcol - NA_COLS // 2, 0, GRID_W - NA_COLS)
    valid = (kcol[None, :] >= qstart[:, None]) & (kcol[None, :] < qstart[:, None] + NA_COLS)
    m = jnp.where(valid, toep, NEG_INF).astype(F32)
    return jnp.concatenate([m[:, :, :ndr - 1], m[:, :, 1:]], axis=-1)


def _span_tables(seq_lens):
    nspan = sum(seq_lens) // SPAN
    first = np.zeros(nspan, np.int32)
    last = np.zeros(nspan, np.int32)
    koff, case = [], []
    s0 = 0
    for L in seq_lens:
        ns = L // SPAN
        first[s0] = 1
        last[s0 + ns - 1] = 1
        s0 += ns
        rows = L // GRID_W
        r = np.arange(rows)
        delta = np.clip(r - NA_ROWS // 2, 0, rows - NA_ROWS) - r
        koff.append((delta + r % ROWS_PER_SPAN) * GRID_W + C_NBR)
        case.append(delta + NA_ROWS - 1)
    flags = jnp.asarray(np.stack([first, last]))
    tab_c = jnp.asarray(np.stack([np.concatenate(koff), np.concatenate(case)]).astype(np.int32))
    return flags, tab_c, nspan


def _qkv_norm_tables(qk_norm):
    segs = [(A_HEADS, 0), (A_HEADS, 1), (A_HEADS, None), (B_Q_HEADS, 2), (B_KV_HEADS, 3), (B_KV_HEADS, None),
            (C_HEADS, 4), (C_HEADS, 5), (C_HEADS, None)]
    gains, flags = [], []
    for nh, idx in segs:
        if idx is None:
            gains.append(jnp.ones((DEPTH, nh * HEAD_DIM), F32))
            flags.append(np.zeros(nh * HEAD_DIM, np.float32))
        else:
            gains.append(jnp.tile(qk_norm[:, idx, :].astype(F32), (1, nh)))
            flags.append(np.ones(nh * HEAD_DIM, np.float32))
    gain = jnp.concatenate(gains, axis=1)[:, None, :]
    flag = jnp.asarray(np.concatenate(flags))[None, :]
    return gain, flag


def kernel(x_prompt, x_sample, rel_bias, ffn1_norm, ffn1_w_in, ffn1_w_out, mix_norm, w_in, qk_norm, sink, rpb,
           w_branch_a, w_branch_b, w_branch_c, w_out, ffn2_norm, ffn2_w_in, ffn2_w_out):
    seq_lens = (x_prompt.shape[1],) * x_prompt.shape[0] + (x_sample.shape[1],) * x_sample.shape[0]
    assert all(L % SPAN == 0 for L in seq_lens)
    x = jnp.concatenate([x_prompt.reshape(-1, D_MODEL), x_sample.reshape(-1, D_MODEL)], axis=0)

    flags, tab_c, nspan = _span_tables(seq_lens)
    bias_a, bias_b = _bias_tables_ab(rel_bias.astype(F32))
    bias_c = _bias_table_c(rpb.astype(F32))
    qk_gain, qk_flag = _qkv_norm_tables(qk_norm)
    rep = B_Q_HEADS // B_KV_HEADS
    sink_rows = jnp.broadcast_to(
        jnp.repeat(sink.astype(F32), B_HALF_WINDOW, axis=1).reshape(DEPTH, B_KV_HEADS, rep * B_HALF_WINDOW, 1),
        (DEPTH, B_KV_HEADS, rep * B_HALF_WINDOW, HEAD_DIM))

    n1 = ffn1_norm.astype(F32)[:, None, :]
    nm = mix_norm.astype(F32)[:, None, :]
    n2 = ffn2_norm.astype(F32)[:, None, :]

    qkv_hosted = (ffn1_w_in, ffn2_w_in, ffn1_w_out, ffn2_w_out)
    mix_hosted = (w_in, w_branch_a, w_branch_b, w_branch_c, w_out)
    f1_in, f2_in, f1_out, f2_out = (w[0].astype(BF16) for w in qkv_hosted)
    w_in_b, w_ba, w_bb, w_bc, w_o = (w[0].astype(BF16) for w in mix_hosted)
    t = x.shape[0]
    qkv_steps = (t // SPAN, 8)
    mix_steps = (t // MIX_TM, D_MODEL // MIX_TN)

    for l in range(DEPTH):
        nxt = l + 1
        qkv_casts, mix_casts = [], []
        if nxt < DEPTH:
            qkv_casts = [_CastJob(w, nxt, qkv_steps) for w in qkv_hosted]
            mix_casts = [_CastJob(w_in, nxt, mix_steps)]
            mix_casts += [_CastJob(w, nxt, (mix_steps[0], 1)) for w in mix_hosted[1:4]]
            mix_casts += [_CastJob(w_out, nxt, mix_steps)]
        x = _ffn(x, n1, f1_in, f1_out, l)
        qkv, next_ffn = _qkv_proj(x, nm, w_in_b, qk_gain, qk_flag, l, qkv_casts)
        o_a = _attn_a(qkv, bias_a, flags, nspan)
        o_b = _attn_b(qkv, bias_b, sink_rows[l], flags, nspan)
        o_c = _attn_c(qkv, bias_c, tab_c, nspan, l)
        x, next_mix = _mix_out(x, nm, o_a, o_b, o_c, w_in_b, w_ba, w_bb, w_bc, w_o, l, mix_casts)
        if nxt < DEPTH:
            x = _ffn(x, n2, f2_in, f2_out, l)
            f1_in, f2_in, f1_out, f2_out = next_ffn
            w_in_b, w_ba, w_bb, w_bc, w_o = next_mix

    n_p = x_prompt.shape[0] * x_prompt.shape[1]
    y_prompt = _ffn(x, n2, f2_in, f2_out, DEPTH - 1, rows=(0, n_p))
    y_sample = _ffn(x, n2, f2_in, f2_out, DEPTH - 1, rows=(n_p, t))
    return (y_prompt.reshape(x_prompt.shape), y_sample.reshape(x_sample.shape))
```

```python
import functools

import numpy as np
import jax
import jax.numpy as jnp
from jax import lax
from jax.experimental import pallas as pl
from jax.experimental.pallas import tpu as pltpu

F32 = jnp.float32
BF16 = jnp.bfloat16

D_MODEL = 2048
DEPTH = 4
HEAD_DIM = 128
DILATED_GROUPS = ((128, 1), (512, 4), (2048, 16))
A_HEADS_PER_GROUP = 4
A_HEADS = 12
A_OUT = 512
B_Q_HEADS = 8
B_KV_HEADS = 2
B_HALF_WINDOW = 128
B_OUT = 1024
C_HEADS = 8
C_OUT = 1024
GRID_W = 64
NA_ROWS = 8
NA_COLS = 16
T5_BUCKETS = 32
T5_MAX_DIST = 1024
D_FF = 5632
EPS = 1e-6
NEG_INF = -1e30
SM_SCALE = HEAD_DIM ** -0.5
QKV_COLS = 9216
N_QKV_HEADS = QKV_COLS // HEAD_DIM
GATE_COL0 = QKV_COLS

SPAN = 1024
A_HW = 64
VMEM_LIMIT = 56 * 1024 * 1024


def _t5_bucket(rel):
    nb = T5_BUCKETS // 2
    max_exact = nb // 2
    sign = (rel > 0).astype(np.int32) * nb
    n = np.abs(rel)
    large = max_exact + (np.log(np.maximum(n, 1) / max_exact) / np.log(T5_MAX_DIST / max_exact)
                         * (nb - max_exact)).astype(np.int32)
    large = np.minimum(large, nb - 1)
    return sign + np.where(n < max_exact, n, large)


def _rms(x, g):
    ms = jnp.mean(x * x, axis=-1, keepdims=True)
    return (x * lax.rsqrt(ms + EPS)) * g


def _skewed(n, first, second, skew):
    pending = [first(i) for i in range(min(skew, n))]
    for i in range(n):
        cur = pending.pop(0)
        if i + skew < n:
            pending.append(first(i + skew))
        second(i, cur)


def _ffn_kernel(x_ref, g_ref, wg_ref, wu_ref, wo_ref, o_ref, h_ref, *, nf, nchunk):
    f = pl.program_id(1)
    rows_per = x_ref.shape[0] // nchunk

    def body(first, last):
        def up_proj(c):
            rows = slice(c * rows_per, (c + 1) * rows_per)
            if first:
                h = _rms(x_ref[rows, :], g_ref[...]).astype(BF16)
                h_ref[rows, :] = h
            else:
                h = h_ref[rows, :]
            gate = jnp.dot(h, wg_ref[...], preferred_element_type=F32)
            up = jnp.dot(h, wu_ref[...], preferred_element_type=F32)
            return gate, up

        def down_proj(c, gate_up):
            rows = slice(c * rows_per, (c + 1) * rows_per)
            gate, up = gate_up
            act = ((gate * jax.nn.sigmoid(gate)) * up).astype(BF16)
            acc = jnp.dot(act, wo_ref[...], preferred_element_type=F32)
            if not first:
                acc = o_ref[rows, :] + acc
            o_ref[rows, :] = (x_ref[rows, :] + 0.5 * acc) if last else acc

        _skewed(nchunk, up_proj, down_proj, skew=1)

    assert nf >= 2
    pl.when(f == 0)(lambda: body(True, False))
    pl.when(jnp.logical_and(f > 0, f < nf - 1))(lambda: body(False, False))
    pl.when(f == nf - 1)(lambda: body(False, True))


FFN_TM = 1024


def _ffn(x, g, w_in, w_out, layer, *, rows=None, tf=512, nchunk=4):
    tm = FFN_TM
    r0, r1 = rows if rows is not None else (0, x.shape[0])
    assert r0 % tm == 0 and r1 % tm == 0
    tile0, t = r0 // tm, r1 - r0
    nf = D_FF // tf
    return pl.pallas_call(
        functools.partial(_ffn_kernel, nf=nf, nchunk=nchunk),
        grid=(t // tm, nf),
        in_specs=[
            pl.BlockSpec((tm, D_MODEL), lambda i, f: (tile0 + i, 0)),
            pl.BlockSpec((None, 1, D_MODEL), lambda i, f: (layer, 0, 0)),
            pl.BlockSpec((D_MODEL, tf), lambda i, f: (0, f)),
            pl.BlockSpec((D_MODEL, tf), lambda i, f: (0, nf + f)),
            pl.BlockSpec((tf, D_MODEL), lambda i, f: (f, 0)),
        ],
        out_specs=pl.BlockSpec((tm, D_MODEL), lambda i, f: (i, 0)),
        out_shape=jax.ShapeDtypeStruct((t, D_MODEL), F32),
        scratch_shapes=[pltpu.VMEM((tm, D_MODEL), BF16)],
        compiler_params=pltpu.CompilerParams(
            dimension_semantics=("parallel", "arbitrary"), vmem_limit_bytes=VMEM_LIMIT),
        name="ffn",
    )(x, g, w_in, w_in, w_out)


class _CastJob:
    def __init__(self, stacked, layer, grid_blocks):
        self.stacked = stacked
        self.layer = layer
        _, rows, cols = stacked.shape
        self.nr, self.nc = grid_blocks
        assert rows % self.nr == 0 and cols % self.nc == 0
        self.block = (rows // self.nr, cols // self.nc)
        assert self.block[0] % 16 == 0 and (self.block[1] % 128 == 0)
        self.out_shape = jax.ShapeDtypeStruct((rows, cols), BF16)

    def _index(self, i, j):
        ci = jnp.minimum(j, self.nc - 1) if self.nc > 1 else 0
        return i, ci

    def in_spec(self):
        return pl.BlockSpec((None,) + self.block, lambda i, j: (self.layer,) + self._index(i, j))

    def out_spec(self):
        return pl.BlockSpec(self.block, lambda i, j: self._index(i, j))


def _run_casts(in_refs, out_refs):
    for src, dst in zip(in_refs, out_refs):
        dst[...] = src[...].astype(BF16)


def _qkv_kernel(x_ref, g_ref, w_ref, gain_ref, flag_ref, *rest, hpb, ncast):
    cast_in = rest[:ncast]
    o_ref = rest[ncast]
    cast_out = rest[ncast + 1:2 * ncast + 1]
    h_ref, stage_ref = rest[2 * ncast + 1:]
    n = pl.program_id(1)
    nchunk = 4
    rows_per = SPAN // nchunk
    quads = hpb // A_HEADS_PER_GROUP

    @pl.when(n == 0)
    def _():
        h_ref[...] = _rms(x_ref[...], g_ref[...]).astype(BF16)

    def project(c):
        return jnp.dot(h_ref[c * rows_per:(c + 1) * rows_per, :], w_ref[...], preferred_element_type=F32)

    def head_norm(c, acc):
        _run_casts(cast_in[c::nchunk], cast_out[c::nchunk])
        rows = slice(c * rows_per, (c + 1) * rows_per)
        for hh in range(hpb):
            sl = slice(hh * HEAD_DIM, (hh + 1) * HEAD_DIM)
            a = acc[:, sl]
            normed = _rms(a, gain_ref[:, sl])
            res = jnp.where(flag_ref[:, sl] > 0.0, normed, a)
            stage_ref[hh, rows, :] = res
            o_ref[hh, rows, :] = res.astype(BF16)

    _skewed(nchunk, project, head_norm, skew=1)

    for qd in range(quads):
        quad = n * quads + qd
        group = jnp.where(quad < 3 * len(DILATED_GROUPS), quad % 3, 0)
        for gi, (_, d) in enumerate(DILATED_GROUPS):
            if d == 1:
                continue

            @pl.when(group == gi)
            def _(d=d, qd=qd):
                per = SPAN // d
                for hh in range(qd * A_HEADS_PER_GROUP, (qd + 1) * A_HEADS_PER_GROUP):
                    for r in range(d):
                        o_ref[hh, r * per:(r + 1) * per, :] = (
                            stage_ref[hh, pl.ds(r, per, stride=d), :].astype(BF16))


def _qkv_proj(x, g, w_in, gain, flag, layer, casts, *, tn=1024):
    t = x.shape[0]
    nn = QKV_COLS // tn
    hpb = tn // HEAD_DIM
    outs = pl.pallas_call(
        functools.partial(_qkv_kernel, hpb=hpb, ncast=len(casts)),
        grid=(t // SPAN, nn),
        in_specs=[
            pl.BlockSpec((SPAN, D_MODEL), lambda i, n: (i, 0)),
            pl.BlockSpec((None, 1, D_MODEL), lambda i, n: (layer, 0, 0)),
            pl.BlockSpec((D_MODEL, tn), lambda i, n: (0, n)),
            pl.BlockSpec((None, 1, tn), lambda i, n: (layer, 0, n)),
            pl.BlockSpec((1, tn), lambda i, n: (0, n)),
        ] + [c.in_spec() for c in casts],
        out_specs=[pl.BlockSpec((hpb, SPAN, HEAD_DIM), lambda i, n: (n, i, 0))] + [c.out_spec() for c in casts],
        out_shape=[jax.ShapeDtypeStruct((N_QKV_HEADS, t, HEAD_DIM), BF16)] + [c.out_shape for c in casts],
        scratch_shapes=[pltpu.VMEM((SPAN, D_MODEL), BF16), pltpu.VMEM((hpb, SPAN, HEAD_DIM), F32)],
        compiler_params=pltpu.CompilerParams(
            dimension_semantics=("parallel", "arbitrary"), vmem_limit_bytes=VMEM_LIMIT),
        name="qkv_proj",
    )(x, g, w_in, gain, flag, *[c.stacked for c in casts])
    return outs[0], outs[1:]


def _attn_a_kernel(flags_ref, *refs):
    q_refs = refs[0:3]
    k_refs = refs[3:12]
    v_refs = refs[12:21]
    bias_ref = refs[21]
    o_ref = refs[22]
    acc_ref, m_ref, l_ref = refs[23:26]

    s_idx = pl.program_id(0)
    is_first = flags_ref[0, s_idx] > 0
    is_last = flags_ref[1, s_idx] > 0
    lane = lax.broadcasted_iota(jnp.int32, (1, 3 * A_HW), 1)
    pen_first = jnp.where(jnp.logical_and(lane < A_HW, is_first), NEG_INF, 0.0).astype(F32)
    pen_last = jnp.where(jnp.logical_and(lane >= 2 * A_HW, is_last), NEG_INF, 0.0).astype(F32)
    nchunk = SPAN // A_HW

    ngroup = len(DILATED_GROUPS)
    hps = q_refs[0].shape[0]

    def chunk_plan(idx):
        hh, rem = divmod(idx, ngroup * nchunk)
        g, c = divmod(rem, nchunk)
        d = DILATED_GROUPS[g][1]
        cpr = nchunk // d
        r, jb = divmod(c, cpr)
        prev = (1, c - 1, False) if jb > 0 else (0, r * cpr + cpr - 1, True)
        nxt = (1, c + 1, False) if jb < cpr - 1 else (2, r * cpr, True)
        if d == 1:
            prev = prev if jb > 0 else (0, 0, True)
            nxt = nxt if jb < cpr - 1 else (2, 0, True)
        return hh, g, d, c, r, jb, prev, nxt

    def cat3(refs, hh, g, c, prev, nxt):
        def chunk(which, cc):
            return refs[3 * g + which][hh, cc * A_HW:(cc + 1) * A_HW, :]
        return jnp.concatenate([chunk(prev[0], prev[1]), chunk(1, c), chunk(nxt[0], nxt[1])], axis=0)

    def scores(idx):
        hh, g, d, c, r, jb, prev, nxt = chunk_plan(idx)
        q = q_refs[g][hh, c * A_HW:(c + 1) * A_HW, :]
        kcat = cat3(k_refs, hh, g, c, prev, nxt)
        s = lax.dot_general(q, kcat, (((1,), (1,)), ((), ())), preferred_element_type=F32)
        s = s * SM_SCALE + bias_ref[g, hh]
        if prev[2]:
            s = s + pen_first
        if nxt[2]:
            s = s + pen_last
        return s

    def finish(idx, s):
        hh, g, d, c, r, jb, prev, nxt = chunk_plan(idx)
        vcat = cat3(v_refs, hh, g, c, prev, nxt)
        m = jnp.max(s, axis=-1, keepdims=True)
        p = jnp.exp(s - m)
        l = jnp.sum(p, axis=-1, keepdims=True)
        acc = jnp.dot(p.astype(BF16), vcat, preferred_element_type=F32)
        if d == 1:
            dst = slice(c * A_HW, (c + 1) * A_HW)
        else:
            dst = pl.ds(jb * A_HW * d + r, A_HW, stride=d)
        slot = hh * ngroup + g
        acc_ref[slot, dst, :] = acc
        m_ref[slot, dst, :] = jnp.broadcast_to(m, (A_HW, HEAD_DIM))
        l_ref[slot, dst, :] = jnp.broadcast_to(l, (A_HW, HEAD_DIM))

    _skewed(hps * ngroup * nchunk, scores, finish, skew=96)

    for hh in range(hps):
        ms = [m_ref[hh * ngroup + g] for g in range(ngroup)]
        m_all = functools.reduce(jnp.maximum, ms)
        ws = [jnp.exp(mg - m_all) for mg in ms]
        numer = sum(ws[g] * acc_ref[hh * ngroup + g] for g in range(ngroup))
        denom = sum(ws[g] * l_ref[hh * ngroup + g] for g in range(ngroup))
        o_ref[:, hh * HEAD_DIM:(hh + 1) * HEAD_DIM] = (numer / denom).astype(BF16)


def _attn_a(qkv, bias, flags, nspan, *, hps=2):
    assert A_HEADS_PER_GROUP % hps == 0 and A_HEADS % hps == 0
    blk = (hps, SPAN, HEAD_DIM)
    gstride = A_HEADS_PER_GROUP // hps

    def q_spec(g):
        return pl.BlockSpec(blk, lambda s, j, fl, g=g: (g * gstride + j, s, 0))

    nchunk = SPAN // A_HW
    chunk_blk = (hps, A_HW, HEAD_DIM)

    def kv_specs(base, g):
        hb = base // hps + g * gstride
        cur = pl.BlockSpec(blk, lambda s, j, fl: (hb + j, s, 0))
        if DILATED_GROUPS[g][1] == 1:
            return [
                pl.BlockSpec(chunk_blk, lambda s, j, fl: (hb + j, jnp.maximum(s * nchunk - 1, 0), 0)),
                cur,
                pl.BlockSpec(chunk_blk, lambda s, j, fl: (
                    hb + j, jnp.minimum((s + 1) * nchunk, nspan * nchunk - 1), 0)),
            ]
        return [
            pl.BlockSpec(blk, lambda s, j, fl: (hb + j, jnp.maximum(s - 1, 0), 0)),
            cur,
            pl.BlockSpec(blk, lambda s, j, fl: (hb + j, jnp.minimum(s + 1, nspan - 1), 0)),
        ]

    in_specs = [q_spec(g) for g in range(3)]
    for base in (A_HEADS, 2 * A_HEADS):
        for g in range(3):
            in_specs += kv_specs(base, g)
    ngroup = len(DILATED_GROUPS)
    in_specs.append(pl.BlockSpec((ngroup, hps, A_HW, 3 * A_HW), lambda s, j, fl: (0, j, 0, 0)))
    grid_spec = pltpu.PrefetchScalarGridSpec(
        num_scalar_prefetch=1,
        grid=(nspan, gstride),
        in_specs=in_specs,
        out_specs=pl.BlockSpec((SPAN, hps * HEAD_DIM), lambda s, j, fl: (s, j)),
        scratch_shapes=[pltpu.VMEM((hps * ngroup, SPAN, HEAD_DIM), F32)] * 3,
    )
    return pl.pallas_call(
        _attn_a_kernel,
        grid_spec=grid_spec,
        out_shape=jax.ShapeDtypeStruct((nspan * SPAN, A_OUT), BF16),
        compiler_params=pltpu.CompilerParams(
            dimension_semantics=("parallel", "arbitrary"), vmem_limit_bytes=VMEM_LIMIT),
        name="attn_dilated",
    )(flags, *([qkv] * 21), bias)


def _attn_b_kernel(flags_ref, q_ref, kp_ref, kc_ref, kn_ref, vp_ref, vc_ref, vn_ref, bias_ref, sink_ref, o_ref):
    hw = B_HALF_WINDOW
    rep = B_Q_HEADS // B_KV_HEADS
    s_idx = pl.program_id(1)
    is_first = flags_ref[0, s_idx] > 0
    is_last = flags_ref[1, s_idx] > 0
    lane = lax.broadcasted_iota(jnp.int32, (1, 3 * hw), 1)
    pen_first = jnp.where(jnp.logical_and(lane < hw, is_first), NEG_INF, 0.0).astype(F32)
    pen_last = jnp.where(jnp.logical_and(lane >= 2 * hw, is_last), NEG_INF, 0.0).astype(F32)
    kfull = jnp.concatenate([kp_ref[...], kc_ref[...], kn_ref[...]], axis=0)
    vfull = jnp.concatenate([vp_ref[...], vc_ref[...], vn_ref[...]], axis=0)
    nblk = SPAN // hw
    wide = (rep * hw, HEAD_DIM)

    def scores(b):
        q = q_ref[:, b * hw:(b + 1) * hw, :].reshape(rep * hw, HEAD_DIM)
        kcat = kfull[b * hw:(b + 3) * hw, :]
        s = lax.dot_general(q, kcat, (((1,), (1,)), ((), ())), preferred_element_type=F32)
        s = s * SM_SCALE + bias_ref[...]
        if b == 0:
            s = s + pen_first
        if b == nblk - 1:
            s = s + pen_last
        return s

    def finish(b, s):
        vcat = vfull[b * hw:(b + 3) * hw, :]
        m = jnp.max(s, axis=-1, keepdims=True)
        p = jnp.exp(s - m)
        den = jnp.broadcast_to(jnp.sum(p, axis=-1, keepdims=True), wide)
        num = jnp.dot(p.astype(BF16), vcat, preferred_element_type=F32)
        m = jnp.broadcast_to(m, wide)
        sink = sink_ref[...]
        m2 = jnp.maximum(m, sink)
        a = jnp.exp(m - m2)
        o = (num * a) / (den * a + jnp.exp(sink - m2))
        for hh in range(rep):
            o_ref[b * hw:(b + 1) * hw, hh * HEAD_DIM:(hh + 1) * HEAD_DIM] = (
                o[hh * hw:(hh + 1) * hw, :].astype(BF16))

    _skewed(nblk, scores, finish, skew=2)


def _attn_b(qkv, bias, sink_rows, flags, nspan):
    hw = B_HALF_WINDOW
    rep = B_Q_HEADS // B_KV_HEADS
    q0 = 3 * A_HEADS
    k0 = q0 + B_Q_HEADS
    v0 = k0 + B_KV_HEADS
    per = SPAN // hw
    nb_total = nspan * per

    def nbr_specs(base):
        return [
            pl.BlockSpec((None, hw, HEAD_DIM), lambda g, s, fl: (base + g, jnp.maximum(s * per - 1, 0), 0)),
            pl.BlockSpec((None, SPAN, HEAD_DIM), lambda g, s, fl: (base + g, s, 0)),
            pl.BlockSpec((None, hw, HEAD_DIM),
                         lambda g, s, fl: (base + g, jnp.minimum(s * per + per, nb_total - 1), 0)),
        ]

    in_specs = [pl.BlockSpec((rep, SPAN, HEAD_DIM), lambda g, s, fl: (q0 // rep + g, s, 0))]
    in_specs += nbr_specs(k0) + nbr_specs(v0)
    in_specs.append(pl.BlockSpec((None, rep * hw, 3 * hw), lambda g, s, fl: (g, 0, 0)))
    in_specs.append(pl.BlockSpec((None, rep * hw, HEAD_DIM), lambda g, s, fl: (g, 0, 0)))
    grid_spec = pltpu.PrefetchScalarGridSpec(
        num_scalar_prefetch=1,
        grid=(B_KV_HEADS, nspan),
        in_specs=in_specs,
        out_specs=pl.BlockSpec((SPAN, rep * HEAD_DIM), lambda g, s, fl: (s, g)),
    )
    return pl.pallas_call(
        _attn_b_kernel,
        grid_spec=grid_spec,
        out_shape=jax.ShapeDtypeStruct((nspan * SPAN, B_OUT), BF16),
        compiler_params=pltpu.CompilerParams(
            dimension_semantics=("parallel", "arbitrary"), vmem_limit_bytes=VMEM_LIMIT),
        name="attn_window",
    )(flags, qkv, qkv, qkv, qkv, qkv, qkv, qkv, bias, sink_rows)


C_KEYS = NA_ROWS * GRID_W
C_NBR = 256
ROWS_PER_SPAN = SPAN // GRID_W


def _attn_c_kernel(tab_ref, q_ref, kp_ref, kc_ref, kn_ref, vp_ref, vc_ref, vn_ref, bias_ref, o_ref,
                   kfull_ref, vfull_ref):
    s_idx = pl.program_id(1)
    kfull_ref[:, 0:C_NBR, :] = kp_ref[...]
    kfull_ref[:, C_NBR:C_NBR + SPAN, :] = kc_ref[...]
    kfull_ref[:, C_NBR + SPAN:, :] = kn_ref[...]
    vfull_ref[:, 0:C_NBR, :] = vp_ref[...]
    vfull_ref[:, C_NBR:C_NBR + SPAN, :] = vc_ref[...]
    vfull_ref[:, C_NBR + SPAN:, :] = vn_ref[...]

    def scores(idx):
        hh, rr = divmod(idx, ROWS_PER_SPAN)
        row = s_idx * ROWS_PER_SPAN + rr
        koff = pl.multiple_of(tab_ref[0, row], GRID_W)
        q = q_ref[hh, rr * GRID_W:(rr + 1) * GRID_W, :]
        kcat = kfull_ref[hh, pl.ds(koff, C_KEYS), :]
        s = lax.dot_general(q, kcat, (((1,), (1,)), ((), ())), preferred_element_type=F32)
        di = tab_ref[1, row]
        bias = jnp.concatenate([bias_ref[hh, di + 2 * kk] for kk in range(NA_ROWS // 2)], axis=1)
        return s * SM_SCALE + bias

    def finish(idx, s):
        hh, rr = divmod(idx, ROWS_PER_SPAN)
        row = s_idx * ROWS_PER_SPAN + rr
        koff = pl.multiple_of(tab_ref[0, row], GRID_W)
        vcat = vfull_ref[hh, pl.ds(koff, C_KEYS), :]
        m = jnp.max(s, axis=-1, keepdims=True)
        p = jnp.exp(s - m)
        l = jnp.sum(p, axis=-1, keepdims=True)
        p = p / l
        o = jnp.dot(p.astype(BF16), vcat, preferred_element_type=F32)
        o_ref[rr * GRID_W:(rr + 1) * GRID_W, hh * HEAD_DIM:(hh + 1) * HEAD_DIM] = o.astype(BF16)

    _skewed(q_ref.shape[0] * ROWS_PER_SPAN, scores, finish, skew=64)


def _attn_c(qkv, bias, tab, nspan, layer, *, hps=4):
    q0 = 3 * A_HEADS + B_Q_HEADS + 2 * B_KV_HEADS
    k0 = q0 + C_HEADS
    v0 = k0 + C_HEADS
    per = SPAN // C_NBR
    nb_total = nspan * per
    assert q0 % hps == 0 and C_HEADS % hps == 0

    def nbr_specs(base):
        return [
            pl.BlockSpec((hps, C_NBR, HEAD_DIM),
                         lambda h, s, tb: (base // hps + h, jnp.maximum(s * per - 1, 0), 0)),
            pl.BlockSpec((hps, SPAN, HEAD_DIM), lambda h, s, tb: (base // hps + h, s, 0)),
            pl.BlockSpec((hps, C_NBR, HEAD_DIM),
                         lambda h, s, tb: (base // hps + h, jnp.minimum(s * per + per, nb_total - 1), 0)),
        ]

    in_specs = [pl.BlockSpec((hps, SPAN, HEAD_DIM), lambda h, s, tb: (q0 // hps + h, s, 0))]
    in_specs += nbr_specs(k0) + nbr_specs(v0)
    in_specs.append(pl.BlockSpec((None, hps, 2 * NA_ROWS - 2, GRID_W, 2 * GRID_W),
                                 lambda h, s, tb: (layer, h, 0, 0, 0)))
    grid_spec = pltpu.PrefetchScalarGridSpec(
        num_scalar_prefetch=1,
        grid=(C_HEADS // hps, nspan),
        in_specs=in_specs,
        out_specs=pl.BlockSpec((SPAN, hps * HEAD_DIM), lambda h, s, tb: (s, h)),
        scratch_shapes=[pltpu.VMEM((hps, SPAN + 2 * C_NBR, HEAD_DIM), BF16)] * 2,
    )
    return pl.pallas_call(
        _attn_c_kernel,
        grid_spec=grid_spec,
        out_shape=jax.ShapeDtypeStruct((nspan * SPAN, C_OUT), BF16),
        compiler_params=pltpu.CompilerParams(
            dimension_semantics=("parallel", "arbitrary"), vmem_limit_bytes=VMEM_LIMIT),
        name="attn_nbr",
    )(tab, qkv, qkv, qkv, qkv, qkv, qkv, qkv, bias)


def _mix_kernel(x_ref, g_ref, oa_ref, ob_ref, oc_ref, wg0_ref, wg1_ref, wg2_ref, wba_ref, wbb_ref, wbc_ref,
                wo_ref, *rest, nc, nchunk, ncast):
    cast_in = rest[:ncast]
    y_ref = rest[ncast]
    cast_out = rest[ncast + 1:2 * ncast + 1]
    h_ref = rest[2 * ncast + 1]
    c = pl.program_id(1)
    rows_per = x_ref.shape[0] // nchunk

    def body(first, last):
        def branch_dots(ci):
            rows = slice(ci * rows_per, (ci + 1) * rows_per)
            if first:
                h = _rms(x_ref[rows, :], g_ref[...]).astype(BF16)
                h_ref[rows, :] = h
            else:
                h = h_ref[rows, :]
            g0 = jnp.dot(h, wg0_ref[...], preferred_element_type=F32)
            g1 = jnp.dot(h, wg1_ref[...], preferred_element_type=F32)
            g2 = jnp.dot(h, wg2_ref[...], preferred_element_type=F32)
            pa = jnp.dot(oa_ref[rows, :], wba_ref[...], preferred_element_type=F32)
            pb = jnp.dot(ob_ref[rows, :], wbb_ref[...], preferred_element_type=F32)
            pc = jnp.dot(oc_ref[rows, :], wbc_ref[...], preferred_element_type=F32)
            return g0, g1, g2, pa, pb, pc

        def mix(ci, vals):
            rows = slice(ci * rows_per, (ci + 1) * rows_per)
            g0, g1, g2, pa, pb, pc = vals
            _run_casts(cast_in[ci::nchunk], cast_out[ci::nchunk])
            z = jax.nn.sigmoid(g0) * pa + jax.nn.sigmoid(g1) * pb + jax.nn.sigmoid(g2) * pc
            acc = jnp.dot(z.astype(BF16), wo_ref[...], preferred_element_type=F32)
            if not first:
                acc = y_ref[rows, :] + acc
            y_ref[rows, :] = (x_ref[rows, :] + acc) if last else acc

        _skewed(nchunk, branch_dots, mix, skew=1)

    assert nc >= 2
    pl.when(c == 0)(lambda: body(True, False))
    pl.when(jnp.logical_and(c > 0, c < nc - 1))(lambda: body(False, False))
    pl.when(c == nc - 1)(lambda: body(False, True))


MIX_TM = 512
MIX_TN = 512


def _mix_out(x, g, o_a, o_b, o_c, w_in, w_ba, w_bb, w_bc, w_o, layer, casts, *, nchunk=2):
    t = x.shape[0]
    tm, tn = MIX_TM, MIX_TN
    nc = D_MODEL // tn
    gate_blk = GATE_COL0 // tn

    def gate_spec(b):
        return pl.BlockSpec((D_MODEL, tn), lambda i, c, b=b: (0, gate_blk + b * nc + c))

    outs = pl.pallas_call(
        functools.partial(_mix_kernel, nc=nc, nchunk=nchunk, ncast=len(casts)),
        grid=(t // tm, nc),
        in_specs=[
            pl.BlockSpec((tm, D_MODEL), lambda i, c: (i, 0)),
            pl.BlockSpec((None, 1, D_MODEL), lambda i, c: (layer, 0, 0)),
            pl.BlockSpec((tm, A_OUT), lambda i, c: (i, 0)),
            pl.BlockSpec((tm, B_OUT), lambda i, c: (i, 0)),
            pl.BlockSpec((tm, C_OUT), lambda i, c: (i, 0)),
            gate_spec(0), gate_spec(1), gate_spec(2),
            pl.BlockSpec((A_OUT, tn), lambda i, c: (0, c)),
            pl.BlockSpec((B_OUT, tn), lambda i, c: (0, c)),
            pl.BlockSpec((C_OUT, tn), lambda i, c: (0, c)),
            pl.BlockSpec((tn, D_MODEL), lambda i, c: (c, 0)),
        ] + [cj.in_spec() for cj in casts],
        out_specs=[pl.BlockSpec((tm, D_MODEL), lambda i, c: (i, 0))] + [cj.out_spec() for cj in casts],
        out_shape=[jax.ShapeDtypeStruct((t, D_MODEL), F32)] + [cj.out_shape for cj in casts],
        scratch_shapes=[pltpu.VMEM((tm, D_MODEL), BF16)],
        compiler_params=pltpu.CompilerParams(
            dimension_semantics=("parallel", "arbitrary"), vmem_limit_bytes=VMEM_LIMIT),
        name="mix_out",
    )(x, g, o_a, o_b, o_c, w_in, w_in, w_in, w_ba, w_bb, w_bc, w_o, *[cj.stacked for cj in casts])
    return outs[0], outs[1:]


def _toeplitz(e, nrows, ncols, shift):
    period = e.shape[-1]
    assert nrows - 1 <= shift and shift + ncols <= period - 1
    flat = jnp.tile(e, (1,) * (e.ndim - 1) + (nrows,))[..., :nrows * (period - 1)]
    return flat.reshape(e.shape[:-1] + (nrows, period - 1))[..., shift:shift + ncols]


def _band_bias(rel_bias, hw, d, heads):
    u = np.arange(4 * hw)
    off = u - (2 * hw - 1)
    e = rel_bias[_t5_bucket(off * d)][:, heads].T
    e = jnp.where(np.abs(off)[None, :] <= hw, e, NEG_INF).astype(F32)
    return _toeplitz(e, hw, 3 * hw, hw - 1)


def _bias_tables_ab(rel_bias):
    bias_a = jnp.stack([
        _band_bias(rel_bias, w // (2 * d), d, slice(g * A_HEADS_PER_GROUP, (g + 1) * A_HEADS_PER_GROUP))
        for g, (w, d) in enumerate(DILATED_GROUPS)])
    hw = B_HALF_WINDOW
    rep = B_Q_HEADS // B_KV_HEADS
    bias_b = _band_bias(rel_bias, hw, 1, slice(A_HEADS, A_HEADS + B_Q_HEADS)).reshape(
        B_KV_HEADS, rep * hw, 3 * hw)
    return bias_a, bias_b


def _bias_table_c(rpb):
    ndr = 2 * NA_ROWS - 1
    ext = GRID_W - NA_COLS
    lead = jnp.broadcast_to(rpb[..., :1], rpb.shape[:-1] + (ext,))
    tail = jnp.broadcast_to(rpb[..., -1:], rpb.shape[:-1] + (ext + 1,))
    e = jnp.concatenate([lead, rpb, tail], axis=-1)
    toep = _toeplitz(e, GRID_W, GRID_W, GRID_W - 1)
    qcol = np.arange(GRID_W)
    kcol = np.arange(GRID_W)
    qstart = np.clip(qcol - NA_COLS // 2, 0, GRID_W - NA_COLS)
    valid = (kcol[None, :] >= qstart[:, None]) & (kcol[None, :] < qstart[:, None] + NA_COLS)
    m = jnp.where(valid, toep, NEG_INF).astype(F32)
    return jnp.concatenate([m[:, :, :ndr - 1], m[:, :, 1:]], axis=-1)


def _span_tables(seq_lens):
    nspan = sum(seq_lens) // SPAN
    first = np.zeros(nspan, np.int32)
    last = np.zeros(nspan, np.int32)
    koff, case = [], []
    s0 = 0
    for L in seq_lens:
        ns = L // SPAN
        first[s0] = 1
        last[s0 + ns - 1] = 1
        s0 += ns
        rows = L // GRID_W
        r = np.arange(rows)
        delta = np.clip(r - NA_ROWS // 2, 0, rows - NA_ROWS) - r
        koff.append((delta + r % ROWS_PER_SPAN) * GRID_W + C_NBR)
        case.append(delta + NA_ROWS - 1)
    flags = jnp.asarray(np.stack([first, last]))
    tab_c = jnp.asarray(np.stack([np.concatenate(koff), np.concatenate(case)]).astype(np.int32))
    return flags, tab_c, nspan


def _qkv_norm_tables(qk_norm):
    segs = [(A_HEADS, 0), (A_HEADS, 1), (A_HEADS, None), (B_Q_HEADS, 2), (B_KV_HEADS, 3), (B_KV_HEADS, None),
            (C_HEADS, 4), (C_HEADS, 5), (C_HEADS, None)]
    gains, flags = [], []
    for nh, idx in segs:
        if idx is None:
            gains.append(jnp.ones((DEPTH, nh * HEAD_DIM), F32))
            flags.append(np.zeros(nh * HEAD_DIM, np.float32))
        else:
            gains.append(jnp.tile(qk_norm[:, idx, :].astype(F32), (1, nh)))
            flags.append(np.ones(nh * HEAD_DIM, np.float32))
    gain = jnp.concatenate(gains, axis=1)[:, None, :]
    flag = jnp.asarray(np.concatenate(flags))[None, :]
    return gain, flag


def kernel(x_prompt, x_sample, rel_bias, ffn1_norm, ffn1_w_in, ffn1_w_out, mix_norm, w_in, qk_norm, sink, rpb,
           w_branch_a, w_branch_b, w_branch_c, w_out, ffn2_norm, ffn2_w_in, ffn2_w_out):
    seq_lens = (x_prompt.shape[1],) * x_prompt.shape[0] + (x_sample.shape[1],) * x_sample.shape[0]
    assert all(L % SPAN == 0 for L in seq_lens)
    x = jnp.concatenate([x_prompt.reshape(-1, D_MODEL), x_sample.reshape(-1, D_MODEL)], axis=0)

    flags, tab_c, nspan = _span_tables(seq_lens)
    bias_a, bias_b = _bias_tables_ab(rel_bias.astype(F32))
    bias_c = _bias_table_c(rpb.astype(F32))
    qk_gain, qk_flag = _qkv_norm_tables(qk_norm)
    rep = B_Q_HEADS // B_KV_HEADS
    sink_rows = jnp.broadcast_to(
        jnp.repeat(sink.astype(F32), B_HALF_WINDOW, axis=1).reshape(DEPTH, B_KV_HEADS, rep * B_HALF_WINDOW, 1),
        (DEPTH, B_KV_HEADS, rep * B_HALF_WINDOW, HEAD_DIM))

    n1 = ffn1_norm.astype(F32)[:, None, :]
    nm = mix_norm.astype(F32)[:, None, :]
    n2 = ffn2_norm.astype(F32)[:, None, :]

    qkv_hosted = (ffn1_w_in, ffn2_w_in, ffn1_w_out, ffn2_w_out)
    mix_hosted = (w_in, w_branch_a, w_branch_b, w_branch_c, w_out)
    f1_in, f2_in, f1_out, f2_out = (w[0].astype(BF16) for w in qkv_hosted)
    w_in_b, w_ba, w_bb, w_bc, w_o = (w[0].astype(BF16) for w in mix_hosted)
    t = x.shape[0]
    qkv_steps = (t // SPAN, 8)
    mix_steps = (t // MIX_TM, D_MODEL // MIX_TN)

    for l in range(DEPTH):
        nxt = l + 1
        qkv_casts, mix_casts = [], []
        if nxt < DEPTH:
            qkv_casts = [_CastJob(w, nxt, qkv_steps) for w in qkv_hosted]
            mix_casts = [_CastJob(w_in, nxt, mix_steps)]
            mix_casts += [_CastJob(w, nxt, (mix_steps[0], 1)) for w in mix_hosted[1:4]]
            mix_casts += [_CastJob(w_out, nxt, mix_steps)]
        x = _ffn(x, n1, f1_in, f1_out, l)
        qkv, next_ffn = _qkv_proj(x, nm, w_in_b, qk_gain, qk_flag, l, qkv_casts)
        o_a = _attn_a(qkv, bias_a, flags, nspan)
        o_b = _attn_b(qkv, bias_b, sink_rows[l], flags, nspan)
        o_c = _attn_c(qkv, bias_c, tab_c, nspan, l)
        x, next_mix = _mix_out(x, nm, o_a, o_b, o_c, w_in_b, w_ba, w_bb, w_bc, w_o, l, mix_casts)
        if nxt < DEPTH:
            x = _ffn(x, n2, f2_in, f2_out, l)
            f1_in, f2_in, f1_out, f2_out = next_ffn
            w_in_b, w_ba, w_bb, w_bc, w_o = next_mix

    n_p = x_prompt.shape[0] * x_prompt.shape[1]
    y_prompt = _ffn(x, n2, f2_in, f2_out, DEPTH - 1, rows=(0, n_p))
    y_sample = _ffn(x, n2, f2_in, f2_out, DEPTH - 1, rows=(n_p, t))
    return (y_prompt.reshape(x_prompt.shape), y_sample.reshape(x_sample.shape))
```

```python
import functools

import numpy as np
import jax
import jax.numpy as jnp
from jax import lax
from jax.experimental import pallas as pl
from jax.experimental.pallas import tpu as pltpu

F32 = jnp.float32
BF16 = jnp.bfloat16

D_MODEL = 2048
DEPTH = 4
HEAD_DIM = 128
DILATED_GROUPS = ((128, 1), (512, 4), (2048, 16))
A_HEADS_PER_GROUP = 4
A_HEADS = 12
A_OUT = 512
B_Q_HEADS = 8
B_KV_HEADS = 2
B_HALF_WINDOW = 128
B_OUT = 1024
C_HEADS = 8
C_OUT = 1024
GRID_W = 64
NA_ROWS = 8
NA_COLS = 16
T5_BUCKETS = 32
T5_MAX_DIST = 1024
D_FF = 5632
EPS = 1e-6
NEG_INF = -1e30
SM_SCALE = HEAD_DIM ** -0.5
QKV_COLS = 9216
N_QKV_HEADS = QKV_COLS // HEAD_DIM
GATE_COL0 = QKV_COLS

SPAN = 1024
A_HW = 64
VMEM_LIMIT = 56 * 1024 * 1024
LANES = 128
BF16_SUBLANE_ROWS = 16


def _t5_bucket(rel):
    nb = T5_BUCKETS // 2
    max_exact = nb // 2
    sign = (rel > 0).astype(np.int32) * nb
    n = np.abs(rel)
    large = max_exact + (np.log(np.maximum(n, 1) / max_exact) / np.log(T5_MAX_DIST / max_exact)
                         * (nb - max_exact)).astype(np.int32)
    large = np.minimum(large, nb - 1)
    return sign + np.where(n < max_exact, n, large)


def _rms(x, g):
    ms = jnp.mean(x * x, axis=-1, keepdims=True)
    return (x * lax.rsqrt(ms + EPS)) * g


def _skewed(n, first, second, skew):
    pending = [first(i) for i in range(min(skew, n))]
    for i in range(n):
        cur = pending.pop(0)
        if i + skew < n:
            pending.append(first(i + skew))
        second(i, cur)


def _ffn_kernel(x_ref, g_ref, wg_ref, wu_ref, wo_ref, o_ref, h_ref, *, nf, nchunk):
    f = pl.program_id(1)
    rows_per = x_ref.shape[0] // nchunk

    def body(first, last):
        def up_proj(c):
            rows = slice(c * rows_per, (c + 1) * rows_per)
            if first:
                h = _rms(x_ref[rows, :], g_ref[...]).astype(BF16)
                h_ref[rows, :] = h
            else:
                h = h_ref[rows, :]
            gate = jnp.dot(h, wg_ref[...], preferred_element_type=F32)
            up = jnp.dot(h, wu_ref[...], preferred_element_type=F32)
            return gate, up

        def down_proj(c, gate_up):
            rows = slice(c * rows_per, (c + 1) * rows_per)
            gate, up = gate_up
            act = ((gate * jax.nn.sigmoid(gate)) * up).astype(BF16)
            acc = jnp.dot(act, wo_ref[...], preferred_element_type=F32)
            if not first:
                acc = o_ref[rows, :] + acc
            o_ref[rows, :] = (x_ref[rows, :] + 0.5 * acc) if last else acc

        _skewed(nchunk, up_proj, down_proj, skew=1)

    assert nf >= 2
    pl.when(f == 0)(lambda: body(True, False))
    pl.when(jnp.logical_and(f > 0, f < nf - 1))(lambda: body(False, False))
    pl.when(f == nf - 1)(lambda: body(False, True))


FFN_TM = 1024


def _ffn(x, g, w_in, w_out, layer, *, rows=None, tf=512, nchunk=4):
    tm = FFN_TM
    r0, r1 = rows if rows is not None else (0, x.shape[0])
    assert r0 % tm == 0 and r1 % tm == 0
    tile0, t = r0 // tm, r1 - r0
    nf = D_FF // tf
    return pl.pallas_call(
        functools.partial(_ffn_kernel, nf=nf, nchunk=nchunk),
        grid=(t // tm, nf),
        in_specs=[
            pl.BlockSpec((tm, D_MODEL), lambda i, f: (tile0 + i, 0)),
            pl.BlockSpec((None, 1, D_MODEL), lambda i, f: (layer, 0, 0)),
            pl.BlockSpec((D_MODEL, tf), lambda i, f: (0, f)),
            pl.BlockSpec((D_MODEL, tf), lambda i, f: (0, nf + f)),
            pl.BlockSpec((tf, D_MODEL), lambda i, f: (f, 0)),
        ],
        out_specs=pl.BlockSpec((tm, D_MODEL), lambda i, f: (i, 0)),
        out_shape=jax.ShapeDtypeStruct((t, D_MODEL), F32),
        scratch_shapes=[pltpu.VMEM((tm, D_MODEL), BF16)],
        compiler_params=pltpu.CompilerParams(
            dimension_semantics=("parallel", "arbitrary"), vmem_limit_bytes=VMEM_LIMIT),
        name="ffn",
    )(x, g, w_in, w_in, w_out)


class _CastJob:
    def __init__(self, stacked, layer, grid_blocks):
        self.stacked = stacked
        self.layer = layer
        _, rows, cols = stacked.shape
        self.nr, self.nc = grid_blocks
        assert rows % self.nr == 0 and cols % self.nc == 0
        self.block = (rows // self.nr, cols // self.nc)
        assert self.block[0] % BF16_SUBLANE_ROWS == 0 and self.block[1] % LANES == 0
        self.out_shape = jax.ShapeDtypeStruct((rows, cols), BF16)

    def _index(self, i, j):
        ci = jnp.minimum(j, self.nc - 1) if self.nc > 1 else 0
        return i, ci

    def in_spec(self):
        return pl.BlockSpec((None,) + self.block, lambda i, j: (self.layer,) + self._index(i, j))

    def out_spec(self):
        return pl.BlockSpec(self.block, lambda i, j: self._index(i, j))


def _run_casts(in_refs, out_refs):
    for src, dst in zip(in_refs, out_refs):
        dst[...] = src[...].astype(BF16)


def _qkv_kernel(x_ref, g_ref, w_ref, gain_ref, flag_ref, *rest, hpb, ncast):
    cast_in = rest[:ncast]
    o_ref = rest[ncast]
    cast_out = rest[ncast + 1:2 * ncast + 1]
    h_ref, stage_ref = rest[2 * ncast + 1:]
    n = pl.program_id(1)
    nchunk = 4
    rows_per = SPAN // nchunk
    quads = hpb // A_HEADS_PER_GROUP

    @pl.when(n == 0)
    def _():
        h_ref[...] = _rms(x_ref[...], g_ref[...]).astype(BF16)

    def project(c):
        return jnp.dot(h_ref[c * rows_per:(c + 1) * rows_per, :], w_ref[...], preferred_element_type=F32)

    def head_norm(c, acc):
        _run_casts(cast_in[c::nchunk], cast_out[c::nchunk])
        rows = slice(c * rows_per, (c + 1) * rows_per)
        for hh in range(hpb):
            sl = slice(hh * HEAD_DIM, (hh + 1) * HEAD_DIM)
            a = acc[:, sl]
            normed = _rms(a, gain_ref[:, sl])
            res = jnp.where(flag_ref[:, sl] > 0.0, normed, a)
            stage_ref[hh, rows, :] = res
            o_ref[hh, rows, :] = res.astype(BF16)

    _skewed(nchunk, project, head_norm, skew=1)

    for qd in range(quads):
        quad = n * quads + qd
        group = jnp.where(quad < 3 * len(DILATED_GROUPS), quad % 3, 0)
        for gi, (_, d) in enumerate(DILATED_GROUPS):
            if d == 1:
                continue

            @pl.when(group == gi)
            def _(d=d, qd=qd):
                for hh in range(qd * A_HEADS_PER_GROUP, (qd + 1) * A_HEADS_PER_GROUP):
                    for c in range(SPAN // A_HW):
                        jb, r = divmod(c, d)
                        o_ref[hh, c * A_HW:(c + 1) * A_HW, :] = (
                            stage_ref[hh, pl.ds(jb * A_HW * d + r, A_HW, stride=d), :].astype(BF16))


def _qkv_proj(x, g, w_in, gain, flag, layer, casts, *, tn=1024):
    t = x.shape[0]
    nn = QKV_COLS // tn
    hpb = tn // HEAD_DIM
    outs = pl.pallas_call(
        functools.partial(_qkv_kernel, hpb=hpb, ncast=len(casts)),
        grid=(t // SPAN, nn),
        in_specs=[
            pl.BlockSpec((SPAN, D_MODEL), lambda i, n: (i, 0)),
            pl.BlockSpec((None, 1, D_MODEL), lambda i, n: (layer, 0, 0)),
            pl.BlockSpec((D_MODEL, tn), lambda i, n: (0, n)),
            pl.BlockSpec((None, 1, tn), lambda i, n: (layer, 0, n)),
            pl.BlockSpec((1, tn), lambda i, n: (0, n)),
        ] + [c.in_spec() for c in casts],
        out_specs=[pl.BlockSpec((hpb, SPAN, HEAD_DIM), lambda i, n: (n, i, 0))] + [c.out_spec() for c in casts],
        out_shape=[jax.ShapeDtypeStruct((N_QKV_HEADS, t, HEAD_DIM), BF16)] + [c.out_shape for c in casts],
        scratch_shapes=[pltpu.VMEM((SPAN, D_MODEL), BF16), pltpu.VMEM((hpb, SPAN, HEAD_DIM), F32)],
        compiler_params=pltpu.CompilerParams(
            dimension_semantics=("parallel", "arbitrary"), vmem_limit_bytes=VMEM_LIMIT),
        name="qkv_proj",
    )(x, g, w_in, gain, flag, *[c.stacked for c in casts])
    return outs[0], outs[1:]


def _attn_a_kernel(flags_ref, *refs):
    q_refs = refs[0:3]
    k_refs = refs[3:12]
    v_refs = refs[12:21]
    bias_ref = refs[21]
    o_ref = refs[22]
    acc_ref, m_ref, l_ref = refs[23:26]

    s_idx = pl.program_id(0)
    is_first = flags_ref[0, s_idx] > 0
    is_last = flags_ref[1, s_idx] > 0
    lane = lax.broadcasted_iota(jnp.int32, (1, 3 * A_HW), 1)
    pen_first = jnp.where(jnp.logical_and(lane < A_HW, is_first), NEG_INF, 0.0).astype(F32)
    pen_last = jnp.where(jnp.logical_and(lane >= 2 * A_HW, is_last), NEG_INF, 0.0).astype(F32)
    nchunk = SPAN // A_HW

    ngroup = len(DILATED_GROUPS)
    hps = q_refs[0].shape[0]

    def chunk_plan(idx):
        hh, rem = divmod(idx, ngroup * nchunk)
        g, c = divmod(rem, nchunk)
        d = DILATED_GROUPS[g][1]
        cpr = nchunk // d
        jb, r = divmod(c, d)
        prev = (1, c - d, False) if jb > 0 else (0, r, True)
        nxt = (1, c + d, False) if jb < cpr - 1 else (2, r, True)
        return hh, g, d, c, r, jb, prev, nxt

    def cat3(refs, hh, g, c, prev, nxt):
        def chunk(which, cc):
            return refs[3 * g + which][hh, cc * A_HW:(cc + 1) * A_HW, :]
        return jnp.concatenate([chunk(prev[0], prev[1]), chunk(1, c), chunk(nxt[0], nxt[1])], axis=0)

    def scores(idx):
        hh, g, d, c, r, jb, prev, nxt = chunk_plan(idx)
        q = q_refs[g][hh, c * A_HW:(c + 1) * A_HW, :]
        kcat = cat3(k_refs, hh, g, c, prev, nxt)
        s = lax.dot_general(q, kcat, (((1,), (1,)), ((), ())), preferred_element_type=F32)
        s = s * SM_SCALE + bias_ref[g, hh]
        if prev[2]:
            s = s + pen_first
        if nxt[2]:
            s = s + pen_last
        return s

    def finish(idx, s):
        hh, g, d, c, r, jb, prev, nxt = chunk_plan(idx)
        vcat = cat3(v_refs, hh, g, c, prev, nxt)
        m = jnp.max(s, axis=-1, keepdims=True)
        p = jnp.exp(s - m)
        l = jnp.sum(p, axis=-1, keepdims=True)
        acc = jnp.dot(p.astype(BF16), vcat, preferred_element_type=F32)
        if d == 1:
            dst = slice(c * A_HW, (c + 1) * A_HW)
        else:
            dst = pl.ds(jb * A_HW * d + r, A_HW, stride=d)
        slot = hh * ngroup + g
        acc_ref[slot, dst, :] = acc
        m_ref[slot, dst, :] = jnp.broadcast_to(m, (A_HW, HEAD_DIM))
        l_ref[slot, dst, :] = jnp.broadcast_to(l, (A_HW, HEAD_DIM))

    _skewed(hps * ngroup * nchunk, scores, finish, skew=96)

    for hh in range(hps):
        ms = [m_ref[hh * ngroup + g] for g in range(ngroup)]
        m_all = functools.reduce(jnp.maximum, ms)
        ws = [jnp.exp(mg - m_all) for mg in ms]
        numer = sum(ws[g] * acc_ref[hh * ngroup + g] for g in range(ngroup))
        denom = sum(ws[g] * l_ref[hh * ngroup + g] for g in range(ngroup))
        o_ref[:, hh * HEAD_DIM:(hh + 1) * HEAD_DIM] = (numer / denom).astype(BF16)


def _attn_a(qkv, bias, flags, nspan, *, hps=2):
    assert A_HEADS_PER_GROUP % hps == 0 and A_HEADS % hps == 0
    blk = (hps, SPAN, HEAD_DIM)
    gstride = A_HEADS_PER_GROUP // hps

    def q_spec(g):
        return pl.BlockSpec(blk, lambda s, j, fl, g=g: (g * gstride + j, s, 0))

    def kv_specs(base, g):
        hb = base // hps + g * gstride
        d = DILATED_GROUPS[g][1]
        nbr_rows = d * A_HW
        per = SPAN // nbr_rows
        nbr_blk = (hps, nbr_rows, HEAD_DIM)
        return [
            pl.BlockSpec(nbr_blk, lambda s, j, fl: (hb + j, jnp.maximum(s * per - 1, 0), 0)),
            pl.BlockSpec(blk, lambda s, j, fl: (hb + j, s, 0)),
            pl.BlockSpec(nbr_blk, lambda s, j, fl: (hb + j, jnp.minimum((s + 1) * per, nspan * per - 1), 0)),
        ]

    in_specs = [q_spec(g) for g in range(3)]
    for base in (A_HEADS, 2 * A_HEADS):
        for g in range(3):
            in_specs += kv_specs(base, g)
    ngroup = len(DILATED_GROUPS)
    in_specs.append(pl.BlockSpec((ngroup, hps, A_HW, 3 * A_HW), lambda s, j, fl: (0, j, 0, 0)))
    grid_spec = pltpu.PrefetchScalarGridSpec(
        num_scalar_prefetch=1,
        grid=(nspan, gstride),
        in_specs=in_specs,
        out_specs=pl.BlockSpec((SPAN, hps * HEAD_DIM), lambda s, j, fl: (s, j)),
        scratch_shapes=[pltpu.VMEM((hps * ngroup, SPAN, HEAD_DIM), F32)] * 3,
    )
    return pl.pallas_call(
        _attn_a_kernel,
        grid_spec=grid_spec,
        out_shape=jax.ShapeDtypeStruct((nspan * SPAN, A_OUT), BF16),
        compiler_params=pltpu.CompilerParams(
            dimension_semantics=("parallel", "arbitrary"), vmem_limit_bytes=VMEM_LIMIT),
        name="attn_dilated",
    )(flags, *([qkv] * 21), bias)


def _attn_b_kernel(flags_ref, q_ref, kp_ref, kc_ref, kn_ref, vp_ref, vc_ref, vn_ref, bias_ref, sink_ref, o_ref):
    hw = B_HALF_WINDOW
    rep = B_Q_HEADS // B_KV_HEADS
    s_idx = pl.program_id(1)
    is_first = flags_ref[0, s_idx] > 0
    is_last = flags_ref[1, s_idx] > 0
    lane = lax.broadcasted_iota(jnp.int32, (1, 3 * hw), 1)
    pen_first = jnp.where(jnp.logical_and(lane < hw, is_first), NEG_INF, 0.0).astype(F32)
    pen_last = jnp.where(jnp.logical_and(lane >= 2 * hw, is_last), NEG_INF, 0.0).astype(F32)
    kfull = jnp.concatenate([kp_ref[...], kc_ref[...], kn_ref[...]], axis=0)
    vfull = jnp.concatenate([vp_ref[...], vc_ref[...], vn_ref[...]], axis=0)
    nblk = SPAN // hw
    wide = (rep * hw, HEAD_DIM)

    def scores(b):
        q = q_ref[:, b * hw:(b + 1) * hw, :].reshape(rep * hw, HEAD_DIM)
        kcat = kfull[b * hw:(b + 3) * hw, :]
        s = lax.dot_general(q, kcat, (((1,), (1,)), ((), ())), preferred_element_type=F32)
        s = s * SM_SCALE + bias_ref[...]
        if b == 0:
            s = s + pen_first
        if b == nblk - 1:
            s = s + pen_last
        return s

    def finish(b, s):
        vcat = vfull[b * hw:(b + 3) * hw, :]
        m = jnp.max(s, axis=-1, keepdims=True)
        p = jnp.exp(s - m)
        den = jnp.broadcast_to(jnp.sum(p, axis=-1, keepdims=True), wide)
        num = jnp.dot(p.astype(BF16), vcat, preferred_element_type=F32)
        m = jnp.broadcast_to(m, wide)
        sink = sink_ref[...]
        m2 = jnp.maximum(m, sink)
        a = jnp.exp(m - m2)
        o = (num * a) / (den * a + jnp.exp(sink - m2))
        for hh in range(rep):
            o_ref[b * hw:(b + 1) * hw, hh * HEAD_DIM:(hh + 1) * HEAD_DIM] = (
                o[hh * hw:(hh + 1) * hw, :].astype(BF16))

    _skewed(nblk, scores, finish, skew=2)


def _attn_b(qkv, bias, sink_rows, flags, nspan):
    hw = B_HALF_WINDOW
    rep = B_Q_HEADS // B_KV_HEADS
    q0 = 3 * A_HEADS
    k0 = q0 + B_Q_HEADS
    v0 = k0 + B_KV_HEADS
    per = SPAN // hw
    nb_total = nspan * per

    def nbr_specs(base):
        return [
            pl.BlockSpec((None, hw, HEAD_DIM), lambda g, s, fl: (base + g, jnp.maximum(s * per - 1, 0), 0)),
            pl.BlockSpec((None, SPAN, HEAD_DIM), lambda g, s, fl: (base + g, s, 0)),
            pl.BlockSpec((None, hw, HEAD_DIM),
                         lambda g, s, fl: (base + g, jnp.minimum(s * per + per, nb_total - 1), 0)),
        ]

    in_specs = [pl.BlockSpec((rep, SPAN, HEAD_DIM), lambda g, s, fl: (q0 // rep + g, s, 0))]
    in_specs += nbr_specs(k0) + nbr_specs(v0)
    in_specs.append(pl.BlockSpec((None, rep * hw, 3 * hw), lambda g, s, fl: (g, 0, 0)))
    in_specs.append(pl.BlockSpec((None, rep * hw, HEAD_DIM), lambda g, s, fl: (g, 0, 0)))
    grid_spec = pltpu.PrefetchScalarGridSpec(
        num_scalar_prefetch=1,
        grid=(B_KV_HEADS, nspan),
        in_specs=in_specs,
        out_specs=pl.BlockSpec((SPAN, rep * HEAD_DIM), lambda g, s, fl: (s, g)),
    )
    return pl.pallas_call(
        _attn_b_kernel,
        grid_spec=grid_spec,
        out_shape=jax.ShapeDtypeStruct((nspan * SPAN, B_OUT), BF16),
        compiler_params=pltpu.CompilerParams(
            dimension_semantics=("parallel", "arbitrary"), vmem_limit_bytes=VMEM_LIMIT),
        name="attn_window",
    )(flags, qkv, qkv, qkv, qkv, qkv, qkv, qkv, bias, sink_rows)


C_KEYS = NA_ROWS * GRID_W
C_NBR = 256
ROWS_PER_SPAN = SPAN // GRID_W


def _attn_c_kernel(tab_ref, q_ref, kp_ref, kc_ref, kn_ref, vp_ref, vc_ref, vn_ref, bias_ref, o_ref,
                   kfull_ref, vfull_ref):
    s_idx = pl.program_id(1)
    kfull_ref[:, 0:C_NBR, :] = kp_ref[...]
    kfull_ref[:, C_NBR:C_NBR + SPAN, :] = kc_ref[...]
    kfull_ref[:, C_NBR + SPAN:, :] = kn_ref[...]
    vfull_ref[:, 0:C_NBR, :] = vp_ref[...]
    vfull_ref[:, C_NBR:C_NBR + SPAN, :] = vc_ref[...]
    vfull_ref[:, C_NBR + SPAN:, :] = vn_ref[...]

    def scores(idx):
        hh, rr = divmod(idx, ROWS_PER_SPAN)
        row = s_idx * ROWS_PER_SPAN + rr
        koff = pl.multiple_of(tab_ref[0, row], GRID_W)
        q = q_ref[hh, rr * GRID_W:(rr + 1) * GRID_W, :]
        kcat = kfull_ref[hh, pl.ds(koff, C_KEYS), :]
        s = lax.dot_general(q, kcat, (((1,), (1,)), ((), ())), preferred_element_type=F32)
        di = tab_ref[1, row]
        bias = jnp.concatenate([bias_ref[hh, di + 2 * kk] for kk in range(NA_ROWS // 2)], axis=1)
        return s * SM_SCALE + bias

    def finish(idx, s):
        hh, rr = divmod(idx, ROWS_PER_SPAN)
        row = s_idx * ROWS_PER_SPAN + rr
        koff = pl.multiple_of(tab_ref[0, row], GRID_W)
        vcat = vfull_ref[hh, pl.ds(koff, C_KEYS), :]
        m = jnp.max(s, axis=-1, keepdims=True)
        p = jnp.exp(s - m)
        l = jnp.sum(p, axis=-1, keepdims=True)
        p = p / l
        o = jnp.dot(p.astype(BF16), vcat, preferred_element_type=F32)
        o_ref[rr * GRID_W:(rr + 1) * GRID_W, hh * HEAD_DIM:(hh + 1) * HEAD_DIM] = o.astype(BF16)

    _skewed(q_ref.shape[0] * ROWS_PER_SPAN, scores, finish, skew=64)


def _attn_c(qkv, bias, tab, nspan, layer, *, hps=4):
    q0 = 3 * A_HEADS + B_Q_HEADS + 2 * B_KV_HEADS
    k0 = q0 + C_HEADS
    v0 = k0 + C_HEADS
    per = SPAN // C_NBR
    nb_total = nspan * per
    assert q0 % hps == 0 and C_HEADS % hps == 0

    def nbr_specs(base):
        return [
            pl.BlockSpec((hps, C_NBR, HEAD_DIM),
                         lambda h, s, tb: (base // hps + h, jnp.maximum(s * per - 1, 0), 0)),
            pl.BlockSpec((hps, SPAN, HEAD_DIM), lambda h, s, tb: (base // hps + h, s, 0)),
            pl.BlockSpec((hps, C_NBR, HEAD_DIM),
                         lambda h, s, tb: (base // hps + h, jnp.minimum(s * per + per, nb_total - 1), 0)),
        ]

    in_specs = [pl.BlockSpec((hps, SPAN, HEAD_DIM), lambda h, s, tb: (q0 // hps + h, s, 0))]
    in_specs += nbr_specs(k0) + nbr_specs(v0)
    in_specs.append(pl.BlockSpec((None, hps, 2 * NA_ROWS - 2, GRID_W, 2 * GRID_W),
                                 lambda h, s, tb: (layer, h, 0, 0, 0)))
    grid_spec = pltpu.PrefetchScalarGridSpec(
        num_scalar_prefetch=1,
        grid=(C_HEADS // hps, nspan),
        in_specs=in_specs,
        out_specs=pl.BlockSpec((SPAN, hps * HEAD_DIM), lambda h, s, tb: (s, h)),
        scratch_shapes=[pltpu.VMEM((hps, SPAN + 2 * C_NBR, HEAD_DIM), BF16)] * 2,
    )
    return pl.pallas_call(
        _attn_c_kernel,
        grid_spec=grid_spec,
        out_shape=jax.ShapeDtypeStruct((nspan * SPAN, C_OUT), BF16),
        compiler_params=pltpu.CompilerParams(
            dimension_semantics=("parallel", "arbitrary"), vmem_limit_bytes=VMEM_LIMIT),
        name="attn_nbr",
    )(tab, qkv, qkv, qkv, qkv, qkv, qkv, qkv, bias)


def _mix_kernel(x_ref, g_ref, oa_ref, ob_ref, oc_ref, wg0_ref, wg1_ref, wg2_ref, wba_ref, wbb_ref, wbc_ref,
                wo_ref, *rest, nc, nchunk, ncast):
    cast_in = rest[:ncast]
    y_ref = rest[ncast]
    cast_out = rest[ncast + 1:2 * ncast + 1]
    h_ref = rest[2 * ncast + 1]
    c = pl.program_id(1)
    rows_per = x_ref.shape[0] // nchunk

    def body(first, last):
        def branch_dots(ci):
            rows = slice(ci * rows_per, (ci + 1) * rows_per)
            if first:
                h = _rms(x_ref[rows, :], g_ref[...]).astype(BF16)
                h_ref[rows, :] = h
            else:
                h = h_ref[rows, :]
            g0 = jnp.dot(h, wg0_ref[...], preferred_element_type=F32)
            g1 = jnp.dot(h, wg1_ref[...], preferred_element_type=F32)
            g2 = jnp.dot(h, wg2_ref[...], preferred_element_type=F32)
            pa = jnp.dot(oa_ref[rows, :], wba_ref[...], preferred_element_type=F32)
            pb = jnp.dot(ob_ref[rows, :], wbb_ref[...], preferred_element_type=F32)
            pc = jnp.dot(oc_ref[rows, :], wbc_ref[...], preferred_element_type=F32)
            return g0, g1, g2, pa, pb, pc

        def mix(ci, vals):
            rows = slice(ci * rows_per, (ci + 1) * rows_per)
            g0, g1, g2, pa, pb, pc = vals
            _run_casts(cast_in[ci::nchunk], cast_out[ci::nchunk])
            z = jax.nn.sigmoid(g0) * pa + jax.nn.sigmoid(g1) * pb + jax.nn.sigmoid(g2) * pc
            acc = jnp.dot(z.astype(BF16), wo_ref[...], preferred_element_type=F32)
            if not first:
                acc = y_ref[rows, :] + acc
            y_ref[rows, :] = (x_ref[rows, :] + acc) if last else acc

        _skewed(nchunk, branch_dots, mix, skew=1)

    assert nc >= 2
    pl.when(c == 0)(lambda: body(True, False))
    pl.when(jnp.logical_and(c > 0, c < nc - 1))(lambda: body(False, False))
    pl.when(c == nc - 1)(lambda: body(False, True))


MIX_TM = 512
MIX_TN = 512


def _mix_out(x, g, o_a, o_b, o_c, w_in, w_ba, w_bb, w_bc, w_o, layer, casts, *, nchunk=2):
    t = x.shape[0]
    tm, tn = MIX_TM, MIX_TN
    nc = D_MODEL // tn
    gate_blk = GATE_COL0 // tn

    def gate_spec(b):
        return pl.BlockSpec((D_MODEL, tn), lambda i, c, b=b: (0, gate_blk + b * nc + c))

    outs = pl.pallas_call(
        functools.partial(_mix_kernel, nc=nc, nchunk=nchunk, ncast=len(casts)),
        grid=(t // tm, nc),
        in_specs=[
            pl.BlockSpec((tm, D_MODEL), lambda i, c: (i, 0)),
            pl.BlockSpec((None, 1, D_MODEL), lambda i, c: (layer, 0, 0)),
            pl.BlockSpec((tm, A_OUT), lambda i, c: (i, 0)),
            pl.BlockSpec((tm, B_OUT), lambda i, c: (i, 0)),
            pl.BlockSpec((tm, C_OUT), lambda i, c: (i, 0)),
            gate_spec(0), gate_spec(1), gate_spec(2),
            pl.BlockSpec((A_OUT, tn), lambda i, c: (0, c)),
            pl.BlockSpec((B_OUT, tn), lambda i, c: (0, c)),
            pl.BlockSpec((C_OUT, tn), lambda i, c: (0, c)),
            pl.BlockSpec((tn, D_MODEL), lambda i, c: (c, 0)),
        ] + [cj.in_spec() for cj in casts],
        out_specs=[pl.BlockSpec((tm, D_MODEL), lambda i, c: (i, 0))] + [cj.out_spec() for cj in casts],
        out_shape=[jax.ShapeDtypeStruct((t, D_MODEL), F32)] + [cj.out_shape for cj in casts],
        scratch_shapes=[pltpu.VMEM((tm, D_MODEL), BF16)],
        compiler_params=pltpu.CompilerParams(
            dimension_semantics=("parallel", "arbitrary"), vmem_limit_bytes=VMEM_LIMIT),
        name="mix_out",
    )(x, g, o_a, o_b, o_c, w_in, w_in, w_in, w_ba, w_bb, w_bc, w_o, *[cj.stacked for cj in casts])
    return outs[0], outs[1:]


def _toeplitz(e, nrows, ncols, shift):
    period = e.shape[-1]
    assert nrows - 1 <= shift and shift + ncols <= period - 1
    flat = jnp.tile(e, (1,) * (e.ndim - 1) + (nrows,))[..., :nrows * (period - 1)]
    return flat.reshape(e.shape[:-1] + (nrows, period - 1))[..., shift:shift + ncols]


def _band_bias(rel_bias, hw, d, heads):
    u = np.arange(4 * hw)
    off = u - (2 * hw - 1)
    e = rel_bias[_t5_bucket(off * d)][:, heads].T
    e = jnp.where(np.abs(off)[None, :] <= hw, e, NEG_INF).astype(F32)
    return _toeplitz(e, hw, 3 * hw, hw - 1)


def _bias_tables_ab(rel_bias):
    bias_a = jnp.stack([
        _band_bias(rel_bias, w // (2 * d), d, slice(g * A_HEADS_PER_GROUP, (g + 1) * A_HEADS_PER_GROUP))
        for g, (w, d) in enumerate(DILATED_GROUPS)])
    hw = B_HALF_WINDOW
    rep = B_Q_HEADS // B_KV_HEADS
    bias_b = _band_bias(rel_bias, hw, 1, slice(A_HEADS, A_HEADS + B_Q_HEADS)).reshape(
        B_KV_HEADS, rep * hw, 3 * hw)
    return bias_a, bias_b


def _bias_table_c(rpb):
    ndr = 2 * NA_ROWS - 1
    ext = GRID_W - NA_COLS
    lead = jnp.broadcast_to(rpb[..., :1], rpb.shape[:-1] + (ext,))
    tail = jnp.broadcast_to(rpb[..., -1:], rpb.shape[:-1] + (ext + 1,))
    e = jnp.concatenate([lead, rpb, tail], axis=-1)
    toep = _toeplitz(e, GRID_W, GRID_W, GRID_W - 1)
    qcol = np.arange(GRID_W)
    kcol = np.arange(GRID_W)
    qstart = np.clip(qcol - NA_COLS // 2, 0, GRID_W - NA_COLS)
    valid = (kcol[None, :] >= qstart[:, None]) & (kcol[None, :] < qstart[:, None] + NA_COLS)
    m = jnp.where(valid, toep, NEG_INF).astype(F32)
    return jnp.concatenate([m[:, :, :ndr - 1], m[:, :, 1:]], axis=-1)


def _span_tables(seq_lens):
    nspan = sum(seq_lens) // SPAN
    first = np.zeros(nspan, np.int32)
    last = np.zeros(nspan, np.int32)
    koff, case = [], []
    s0 = 0
    for L in seq_lens:
        ns = L // SPAN
        first[s0] = 1
        last[s0 + ns - 1] = 1
        s0 += ns
        rows = L // GRID_W
        r = np.arange(rows)
        delta = np.clip(r - NA_ROWS // 2, 0, rows - NA_ROWS) - r
        koff.append((delta + r % ROWS_PER_SPAN) * GRID_W + C_NBR)
        case.append(delta + NA_ROWS - 1)
    flags = jnp.asarray(np.stack([first, last]))
    tab_c = jnp.asarray(np.stack([np.concatenate(koff), np.concatenate(case)]).astype(np.int32))
    return flags, tab_c, nspan


def _qkv_norm_tables(qk_norm):
    segs = [(A_HEADS, 0), (A_HEADS, 1), (A_HEADS, None), (B_Q_HEADS, 2), (B_KV_HEADS, 3), (B_KV_HEADS, None),
            (C_HEADS, 4), (C_HEADS, 5), (C_HEADS, None)]
    gains, flags = [], []
    for nh, idx in segs:
        if idx is None:
            gains.append(jnp.ones((DEPTH, nh * HEAD_DIM), F32))
            flags.append(np.zeros(nh * HEAD_DIM, np.float32))
        else:
            gains.append(jnp.tile(qk_norm[:, idx, :].astype(F32), (1, nh)))
            flags.append(np.ones(nh * HEAD_DIM, np.float32))
    gain = jnp.concatenate(gains, axis=1)[:, None, :]
    flag = jnp.asarray(np.concatenate(flags))[None, :]
    return gain, flag


def kernel(x_prompt, x_sample, rel_bias, ffn1_norm, ffn1_w_in, ffn1_w_out, mix_norm, w_in, qk_norm, sink, rpb,
           w_branch_a, w_branch_b, w_branch_c, w_out, ffn2_norm, ffn2_w_in, ffn2_w_out):
    seq_lens = (x_prompt.shape[1],) * x_prompt.shape[0] + (x_sample.shape[1],) * x_sample.shape[0]
    assert all(L % SPAN == 0 for L in seq_lens)
    x = jnp.concatenate([x_prompt.reshape(-1, D_MODEL), x_sample.reshape(-1, D_MODEL)], axis=0)

    flags, tab_c, nspan = _span_tables(seq_lens)
    bias_a, bias_b = _bias_tables_ab(rel_bias.astype(F32))
    bias_c = _bias_table_c(rpb.astype(F32))
    qk_gain, qk_flag = _qkv_norm_tables(qk_norm)
    rep = B_Q_HEADS // B_KV_HEADS
    sink_rows = jnp.broadcast_to(
        jnp.repeat(sink.astype(F32), B_HALF_WINDOW, axis=1).reshape(DEPTH, B_KV_HEADS, rep * B_HALF_WINDOW, 1),
        (DEPTH, B_KV_HEADS, rep * B_HALF_WINDOW, HEAD_DIM))

    n1 = ffn1_norm.astype(F32)[:, None, :]
    nm = mix_norm.astype(F32)[:, None, :]
    n2 = ffn2_norm.astype(F32)[:, None, :]

    qkv_hosted = (ffn1_w_in, ffn2_w_in, ffn1_w_out, ffn2_w_out)
    mix_hosted = (w_in, w_branch_a, w_branch_b, w_branch_c, w_out)
    f1_in, f2_in, f1_out, f2_out = (w[0].astype(BF16) for w in qkv_hosted)
    w_in_b, w_ba, w_bb, w_bc, w_o = (w[0].astype(BF16) for w in mix_hosted)
    t = x.shape[0]
    qkv_steps = (t // SPAN, 8)
    mix_steps = (t // MIX_TM, D_MODEL // MIX_TN)

    for l in range(DEPTH):
        nxt = l + 1
        qkv_casts, mix_casts = [], []
        if nxt < DEPTH:
            qkv_casts = [_CastJob(w, nxt, qkv_steps) for w in qkv_hosted]
            mix_casts = [_CastJob(w_in, nxt, mix_steps)]
            mix_casts += [_CastJob(w, nxt, (mix_steps[0], 1)) for w in mix_hosted[1:4]]
            mix_casts += [_CastJob(w_out, nxt, mix_steps)]
        x = _ffn(x, n1, f1_in, f1_out, l)
        qkv, next_ffn = _qkv_proj(x, nm, w_in_b, qk_gain, qk_flag, l, qkv_casts)
        o_a = _attn_a(qkv, bias_a, flags, nspan)
        o_b = _attn_b(qkv, bias_b, sink_rows[l], flags, nspan)
        o_c = _attn_c(qkv, bias_c, tab_c, nspan, l)
        x, next_mix = _mix_out(x, nm, o_a, o_b, o_c, w_in_b, w_ba, w_bb, w_bc, w_o, l, mix_casts)
        if nxt < DEPTH:
            x = _ffn(x, n2, f2_in, f2_out, l)
            f1_in, f2_in, f1_out, f2_out = next_ffn
            w_in_b, w_ba, w_bb, w_bc, w_o = next_mix

    n_p = x_prompt.shape[0] * x_prompt.shape[1]
    y_prompt = _ffn(x, n2, f2_in, f2_out, DEPTH - 1, rows=(0, n_p))
    y_sample = _ffn(x, n2, f2_in, f2_out, DEPTH - 1, rows=(n_p, t))
    return (y_prompt.reshape(x_prompt.shape), y_sample.reshape(x_sample.shape))
```
